```python
import math
import jax
import jax.numpy as jnp
from jax import lax
import numpy as np

D_MODEL = 1024
BATCH = 8
SEQ = 4096
DEPTH = 2

N_META = 16
D_MIX = D_MODEL
D_SSM = D_MIX // 2
D_ATTN = D_MIX - D_SSM
SSM_GROUP = 16
N_SSM_GROUPS = D_SSM // SSM_GROUP
SSM_STATE = 64
HEAD_DIM = 64
N_HEADS = D_ATTN // HEAD_DIM
Q_BLOCK = 128
N_EXPERTS = 32
TOP_K = 4
D_FF = D_MODEL
SWIGLU_LIMIT = 7.0
SWIGLU_ALPHA = 1.702
RMS_EPS = 1e-6
D_IN = D_SSM + 3 * D_ATTN + N_HEADS

kernel_name = "hymba_s5_fox_moe_trunk"


def rmsnorm(x, g):
    xf = x.astype(jnp.float32)
    y = xf * lax.rsqrt(jnp.mean(xf * xf, axis=-1, keepdims=True) + RMS_EPS)
    return (y * g.astype(jnp.float32)).astype(x.dtype)


def _ssm_combine(e1, e2):
    a1, b1 = e1
    a2, b2 = e2
    return a1 * a2, a2 * b1 + b2


def s5_ssm(u, lam_re, lam_im, log_dt, b_re, b_im, c_re, c_im, d):
    bsz, L, _ = u.shape
    uf = u.astype(jnp.float32).reshape(bsz, L, N_SSM_GROUPS, SSM_GROUP)
    lam = lax.complex(lam_re.astype(jnp.float32), lam_im.astype(jnp.float32))
    dt = jnp.exp(log_dt.astype(jnp.float32))[:, None]
    lam_bar = jnp.exp(lam * dt)
    b = lax.complex(b_re.astype(jnp.float32), b_im.astype(jnp.float32))
    b_bar = ((lam_bar - 1.0) / lam)[:, :, None] * b
    c = lax.complex(c_re.astype(jnp.float32), c_im.astype(jnp.float32))
    bu = jnp.einsum("blgh,gph->blgp", uf.astype(jnp.complex64), b_bar)
    a = jnp.broadcast_to(lam_bar[None, None], (1, L, N_SSM_GROUPS, SSM_STATE))
    _, state = lax.associative_scan(_ssm_combine, (a, bu), axis=1)
    y = jnp.einsum("blgp,ghp->blgh", state, c).real + d.astype(jnp.float32).reshape(N_SSM_GROUPS, SSM_GROUP) * uf
    return y.reshape(bsz, L, D_SSM).astype(u.dtype)


def forgetting_attention(q, k, v, log_f):
    L = q.shape[1]
    n_real_blocks = (L - N_META) // Q_BLOCK
    c = jnp.cumsum(log_f, axis=1).transpose(0, 2, 1)
    scale = HEAD_DIM ** -0.5
    neg = jnp.finfo(jnp.float32).min
    bounds = [(0, N_META)] + [(N_META + i * Q_BLOCK, N_META + (i + 1) * Q_BLOCK) for i in range(n_real_blocks)]
    outs = []
    for qs, qe in bounds:
        s = jnp.einsum("bqhd,bkhd->bhqk", q[:, qs:qe], k[:, :qe], preferred_element_type=jnp.float32) * scale
        s = s + c[:, :, qs:qe, None] - c[:, :, None, :qe]
        causal = jnp.arange(qs, qe)[:, None] >= jnp.arange(qe)[None, :]
        s = jnp.where(causal, s, neg)
        p = jax.nn.softmax(s, axis=-1)
        outs.append(jnp.einsum("bhqk,bkhd->bqhd", p.astype(v.dtype), v[:, :qe]))
    return jnp.concatenate(outs, axis=1)


def moe_ffn(h, w_router, b_router, w_gate, b_gate, w_up, b_up, w_down, b_down):
    bsz, L, D = h.shape
    t = h.reshape(bsz * L, D)
    logits = (t @ w_router).astype(jnp.float32) + b_router.astype(jnp.float32)
    top_v, top_i = lax.top_k(logits, TOP_K)
    top_w = jax.nn.softmax(top_v, axis=-1)
    gates = jnp.sum(jax.nn.one_hot(top_i, N_EXPERTS, dtype=jnp.float32) * top_w[..., None], axis=1).astype(h.dtype)
    out = jnp.zeros_like(t)
    for e in range(N_EXPERTS):
        g = jnp.minimum(t @ w_gate[e] + b_gate[e], SWIGLU_LIMIT)
        u = jnp.clip(t @ w_up[e] + b_up[e], -SWIGLU_LIMIT, SWIGLU_LIMIT)
        y = (g * jax.nn.sigmoid(SWIGLU_ALPHA * g) * (u + 1.0)) @ w_down[e] + b_down[e]
        out = out + gates[:, e:e + 1] * y
    return out.reshape(bsz, L, D)


def setup_inputs(seed: int = 0) -> dict:
    key = jax.random.key(seed)
    ks = jax.random.split(key, 28)
    f32 = jnp.float32

    def nrm(k, shape, scale):
        return jax.random.normal(k, shape, f32) * scale

    G, P, H = N_SSM_GROUPS, SSM_STATE, SSM_GROUP
    return {
        "x": nrm(ks[0], (BATCH, SEQ, D_MODEL), 1.0),
        "meta_tokens": nrm(ks[1], (N_META, D_MODEL), 1.0),
        "norm_mix": 1.0 + nrm(ks[2], (DEPTH, D_MODEL), 0.02),
        "w_in": nrm(ks[3], (DEPTH, D_MODEL, D_IN), D_MODEL ** -0.5),
        "ssm_lambda_re": -0.5 * jnp.exp(nrm(ks[4], (DEPTH, G, P), 0.05)),
        "ssm_lambda_im": math.pi * jnp.arange(P, dtype=f32) + nrm(ks[5], (DEPTH, G, P), 0.05),
        "ssm_log_dt": jax.random.uniform(ks[6], (DEPTH, G), f32, math.log(1e-3), math.log(1e-1)),
        "ssm_b_re": nrm(ks[7], (DEPTH, G, P, H), (2.0 * H) ** -0.5),
        "ssm_b_im": nrm(ks[8], (DEPTH, G, P, H), (2.0 * H) ** -0.5),
        "ssm_c_re": nrm(ks[9], (DEPTH, G, H, P), (2.0 * P) ** -0.5),
        "ssm_c_im": nrm(ks[10], (DEPTH, G, H, P), (2.0 * P) ** -0.5),
        "ssm_d": nrm(ks[11], (DEPTH, D_SSM), 1.0),
        "w_glu": nrm(ks[12], (DEPTH, D_SSM, D_SSM), D_SSM ** -0.5),
        "b_glu": nrm(ks[13], (DEPTH, D_SSM), 0.02),
        "b_forget": jax.random.uniform(ks[14], (DEPTH, N_HEADS), f32, 1.0, 5.0),
        "norm_ssm_out": 1.0 + nrm(ks[15], (DEPTH, D_SSM), 0.02),
        "norm_attn_out": 1.0 + nrm(ks[16], (DEPTH, D_ATTN), 0.02),
        "w_out": nrm(ks[17], (DEPTH, D_MIX, D_MODEL), D_MIX ** -0.5),
        "norm_ffn": 1.0 + nrm(ks[18], (DEPTH, D_MODEL), 0.02),
        "w_router": nrm(ks[19], (DEPTH, D_MODEL, N_EXPERTS), D_MODEL ** -0.5),
        "b_router": nrm(ks[20], (DEPTH, N_EXPERTS), 0.01),
        "w_gate": nrm(ks[21], (DEPTH, N_EXPERTS, D_MODEL, D_FF), D_MODEL ** -0.5),
        "b_gate": nrm(ks[22], (DEPTH, N_EXPERTS, D_FF), 0.02),
        "w_up": nrm(ks[23], (DEPTH, N_EXPERTS, D_MODEL, D_FF), D_MODEL ** -0.5),
        "b_up": nrm(ks[24], (DEPTH, N_EXPERTS, D_FF), 0.02),
        "w_down": nrm(ks[25], (DEPTH, N_EXPERTS, D_FF, D_MODEL), D_FF ** -0.5),
        "b_down": nrm(ks[26], (DEPTH, N_EXPERTS, D_MODEL), 0.02),
        "norm_final": 1.0 + nrm(ks[27], (D_MODEL,), 0.02),
    }


def reference(x, meta_tokens, norm_mix, w_in, ssm_lambda_re, ssm_lambda_im, ssm_log_dt,
              ssm_b_re, ssm_b_im, ssm_c_re, ssm_c_im, ssm_d, w_glu, b_glu, b_forget,
              norm_ssm_out, norm_attn_out, w_out, norm_ffn, w_router, b_router,
              w_gate, b_gate, w_up, b_up, w_down, b_down, norm_final):
    bsz = x.shape[0]
    meta = jnp.broadcast_to(meta_tokens[None].astype(x.dtype), (bsz, N_META, D_MODEL))
    h_res = jnp.concatenate([meta, x], axis=1)
    L = h_res.shape[1]
    o_q, o_k, o_v, o_f = D_SSM, D_SSM + D_ATTN, D_SSM + 2 * D_ATTN, D_SSM + 3 * D_ATTN
    for l in range(DEPTH):
        h = rmsnorm(h_res, norm_mix[l])
        proj = h @ w_in[l]
        u = proj[..., :o_q]
        q = proj[..., o_q:o_k].reshape(bsz, L, N_HEADS, HEAD_DIM)
        k = proj[..., o_k:o_v].reshape(bsz, L, N_HEADS, HEAD_DIM)
        v = proj[..., o_v:o_f].reshape(bsz, L, N_HEADS, HEAD_DIM)
        f_logit = proj[..., o_f:]

        y_ssm = s5_ssm(u, ssm_lambda_re[l], ssm_lambda_im[l], ssm_log_dt[l],
                       ssm_b_re[l], ssm_b_im[l], ssm_c_re[l], ssm_c_im[l], ssm_d[l])
        g = jax.nn.gelu(y_ssm)
        y_ssm = g * jax.nn.sigmoid(g @ w_glu[l] + b_glu[l])

        log_f = jax.nn.log_sigmoid(f_logit.astype(jnp.float32) + b_forget[l].astype(jnp.float32))
        y_attn = forgetting_attention(q, k, v, log_f).reshape(bsz, L, D_ATTN)

        mixed = jnp.concatenate([rmsnorm(y_ssm, norm_ssm_out[l]), rmsnorm(y_attn, norm_attn_out[l])], axis=-1)
        h_res = h_res + mixed @ w_out[l]

        h_res = h_res + moe_ffn(rmsnorm(h_res, norm_ffn[l]), w_router[l], b_router[l],
                                w_gate[l], b_gate[l], w_up[l], b_up[l], w_down[l], b_down[l])
    return rmsnorm(h_res[:, N_META:], norm_final)
```

```python
import functools
import math

import jax
import jax.numpy as jnp
from jax import lax
from jax.experimental import pallas as pl
from jax.experimental.pallas import tpu as pltpu

F32 = jnp.float32
BF16 = jnp.bfloat16
I32 = jnp.int32

N_META = 16
HEAD_DIM = 64
N_HEADS = 8
D_ATTN = N_HEADS * HEAD_DIM
D_SSM = 512
SSM_GROUP = 16
SSM_STATE = 64
N_GROUPS = D_SSM // SSM_GROUP
N_EXPERTS = 32
TOP_K = 4
SWIGLU_LIMIT = 7.0
SWIGLU_ALPHA = 1.702
RMS_EPS = 1e-6

LANES = 128
SEQ_ALIGN = 256
SSM_CHUNK = 16
GROUPS_PER_BUNDLE = LANES // SSM_GROUP
N_BUNDLES = N_GROUPS // GROUPS_PER_BUNDLE
BUNDLE_IN = SSM_CHUNK * LANES
BUNDLE_STATE = 2 * GROUPS_PER_BUNDLE * SSM_STATE
EXPERT_TILE = 256
VMEM_LIMIT = 56 * 1024 * 1024
NEG_BIG = -1e30

_NT = (((1,), (1,)), ((), ()))


def _cparams(*sem):
    return pltpu.CompilerParams(dimension_semantics=sem, vmem_limit_bytes=VMEM_LIMIT)


def _rms(x, g):
    return x * lax.rsqrt(jnp.mean(x * x, axis=-1, keepdims=True) + RMS_EPS) * g


def _sigmoid(x):
    return 1.0 / (1.0 + jnp.exp(-x))


def _in_proj_kernel(h_ref, g_ref, w_ref, bf_ref, u_ref, q_ref, k_ref, v_ref, lf_ref):
    xb = _rms(h_ref[...], g_ref[...]).astype(BF16)
    proj = jnp.dot(xb, w_ref[...], preferred_element_type=F32)
    for gb in range(N_BUNDLES):
        u_ref[gb] = proj[:, gb * LANES:(gb + 1) * LANES].astype(BF16)
    o_q, o_k, o_v, o_f = D_SSM, D_SSM + D_ATTN, D_SSM + 2 * D_ATTN, D_SSM + 3 * D_ATTN
    q_ref[...] = (proj[:, o_q:o_k] * (HEAD_DIM ** -0.5)).astype(BF16)
    k_ref[...] = proj[:, o_k:o_v].astype(BF16)
    v_ref[...] = proj[:, o_v:o_f].astype(BF16)
    z = proj[:, o_f:o_f + LANES] + bf_ref[...]
    lf_ref[...] = jnp.minimum(z, 0.0) - jnp.log(1.0 + jnp.exp(-jnp.abs(z)))


def _in_proj(h, g, w, bf, tm):
    tp, d = h.shape
    n = w.shape[1]
    return pl.pallas_call(
        _in_proj_kernel,
        grid=(tp // tm,),
        in_specs=[
            pl.BlockSpec((tm, d), lambda i: (i, 0)),
            pl.BlockSpec((1, d), lambda i: (0, 0)),
            pl.BlockSpec((d, n), lambda i: (0, 0)),
            pl.BlockSpec((1, LANES), lambda i: (0, 0)),
        ],
        out_specs=[
            pl.BlockSpec((N_BUNDLES, tm, LANES), lambda i: (0, i, 0)),
            pl.BlockSpec((tm, D_ATTN), lambda i: (i, 0)),
            pl.BlockSpec((tm, D_ATTN), lambda i: (i, 0)),
            pl.BlockSpec((tm, D_ATTN), lambda i: (i, 0)),
            pl.BlockSpec((tm, LANES), lambda i: (i, 0)),
        ],
        out_shape=[
            jax.ShapeDtypeStruct((N_BUNDLES, tp, LANES), BF16),
            jax.ShapeDtypeStruct((tp, D_ATTN), BF16),
            jax.ShapeDtypeStruct((tp, D_ATTN), BF16),
            jax.ShapeDtypeStruct((tp, D_ATTN), BF16),
            jax.ShapeDtypeStruct((tp, LANES), F32),
        ],
        compiler_params=_cparams("parallel"),
        name="in_proj",
    )(h, g, w, bf)


def _ssm_kernel(x_ref, wt_ref, wb_ref, wc_ref, a_ref, o_ref, z_ref, s_ref):
    x = x_ref[...]
    nc = x.shape[0]
    half = BUNDLE_STATE // 2
    z_ref[...] = jnp.dot(x, wb_ref[...], preferred_element_type=F32)
    a_re = a_ref[:, :half]
    a_im = a_ref[:, half:]

    def step(c8, carry):
        s_re, s_im = carry
        start = pl.multiple_of(c8 * 8, 8)
        zblk = z_ref[pl.ds(start, 8), :]
        rows = []
        for r in range(8):
            rows.append(jnp.concatenate([s_re, s_im], axis=1))
            z_re = zblk[r:r + 1, :half]
            z_im = zblk[r:r + 1, half:]
            s_re, s_im = (a_re * s_re - a_im * s_im + z_re,
                          a_re * s_im + a_im * s_re + z_im)
        s_ref[pl.ds(start, 8), :] = jnp.concatenate(rows, axis=0)
        return s_re, s_im

    zero = jnp.zeros((1, half), F32)
    lax.fori_loop(0, nc // 8, step, (zero, zero))
    y = jnp.dot(x, wt_ref[...], preferred_element_type=F32)
    y = y + jnp.dot(s_ref[...].astype(BF16), wc_ref[...], preferred_element_type=F32)
    o_ref[...] = y.astype(o_ref.dtype)


def _ssm(xb, wt, wb, wc, a16):
    nb, bsz, nc, _ = xb.shape
    return pl.pallas_call(
        _ssm_kernel,
        grid=(nb, bsz),
        in_specs=[
            pl.BlockSpec((None, None, nc, BUNDLE_IN), lambda g, b: (g, b, 0, 0)),
            pl.BlockSpec((None, BUNDLE_IN, BUNDLE_IN), lambda g, b: (g, 0, 0)),
            pl.BlockSpec((None, BUNDLE_IN, BUNDLE_STATE), lambda g, b: (g, 0, 0)),
            pl.BlockSpec((None, BUNDLE_STATE, BUNDLE_IN), lambda g, b: (g, 0, 0)),
            pl.BlockSpec((None, 1, BUNDLE_STATE), lambda g, b: (g, 0, 0)),
        ],
        out_specs=pl.BlockSpec((None, None, nc, BUNDLE_IN), lambda g, b: (g, b, 0, 0)),
        out_shape=jax.ShapeDtypeStruct((nb, bsz, nc, BUNDLE_IN), BF16),
        scratch_shapes=[pltpu.VMEM((nc, BUNDLE_STATE), F32), pltpu.VMEM((nc, BUNDLE_STATE), F32)],
        compiler_params=_cparams("parallel", "parallel"),
        name="s5_chunked",
    )(xb, wt, wb, wc, a16)


def _ssm_operators(lam_re, lam_im, log_dt, b_re, b_im, c_re, c_im, d):
    g_, p_, h_, q_ = N_GROUPS, SSM_STATE, SSM_GROUP, SSM_CHUNK
    nb, g8 = N_BUNDLES, GROUPS_PER_BUNDLE
    lam = lax.complex(lam_re.astype(F32), lam_im.astype(F32))
    dt = jnp.exp(log_dt.astype(F32))[:, None]
    lam_bar = jnp.exp(lam * dt)
    b_bar = ((lam_bar - 1.0) / lam)[:, :, None] * lax.complex(b_re.astype(F32), b_im.astype(F32))
    c = lax.complex(c_re.astype(F32), c_im.astype(F32))
    steps = jnp.arange(q_ + 1, dtype=F32)[:, None, None]
    pw = jnp.exp((lam * dt)[None] * steps)
    eye8 = jnp.eye(g8, dtype=F32)

    kd = jnp.real(jnp.einsum("ghp,kgp,gpi->kghi", c, pw[:q_], b_bar))
    delta = jnp.arange(q_)[None, :] - jnp.arange(q_)[:, None]
    kt = jnp.where((delta >= 0)[:, :, None, None, None], kd[jnp.clip(delta, 0, q_ - 1)], 0.0)
    skip = jnp.eye(h_, dtype=F32)[None] * d.astype(F32).reshape(g_, 1, h_)
    kt = kt + (delta == 0).astype(F32)[:, :, None, None, None] * skip[None, None]
    kt = kt.reshape(q_, q_, nb, g8, h_, h_)
    wt = jnp.einsum("abcdef,dg->cadfbge", kt, eye8).reshape(nb, BUNDLE_IN, BUNDLE_IN)

    bp = pw[q_ - 1 - jnp.arange(q_)][:, :, :, None] * b_bar[None]
    bp = jnp.stack([jnp.real(bp), jnp.imag(bp)]).reshape(2, q_, nb, g8, p_, h_)
    wb = jnp.einsum("racdpf,dg->cadfrgp", bp, eye8).reshape(nb, BUNDLE_IN, BUNDLE_STATE)

    cq = c[None] * pw[1:q_ + 1][:, :, None, :]
    cq = jnp.stack([jnp.real(cq), -jnp.imag(cq)]).reshape(2, q_, nb, g8, h_, p_)
    wc = jnp.einsum("rbcdep,dg->crdpbge", cq, eye8).reshape(nb, BUNDLE_STATE, BUNDLE_IN)

    a16 = jnp.stack([jnp.real(pw[q_]), jnp.imag(pw[q_])]).reshape(2, nb, g8 * p_)
    a16 = a16.transpose(1, 0, 2).reshape(nb, 1, BUNDLE_STATE)
    return wt.astype(BF16), wb.astype(BF16), wc.astype(BF16), a16


def _cumsum_kernel(lf_ref, tri_ref, cc_ref, cr_ref):
    nk, _, tk = cr_ref.shape
    tri = tri_ref[...]
    carry = jnp.zeros((1, LANES), F32)
    for ci in range(nk):
        x = lf_ref[ci * tk:(ci + 1) * tk, :]
        x1 = x.astype(BF16)
        r1 = x - x1.astype(F32)
        x2 = r1.astype(BF16)
        x3 = (r1 - x2.astype(F32)).astype(BF16)
        cs = (jnp.dot(tri, x1, preferred_element_type=F32)
              + jnp.dot(tri, x2, preferred_element_type=F32)
              + jnp.dot(tri, x3, preferred_element_type=F32)) + carry
        cc_ref[ci * tk:(ci + 1) * tk, :] = cs
        cr_ref[ci] = cs.T[:N_HEADS, :]
        carry = cs[tk - 1:tk, :]


def _forget_cumsum(lf, tk):
    bsz, lp, _ = lf.shape
    nk = lp // tk
    tri = (jnp.arange(tk)[:, None] >= jnp.arange(tk)[None, :]).astype(BF16)
    return pl.pallas_call(
        _cumsum_kernel,
        grid=(bsz,),
        in_specs=[
            pl.BlockSpec((None, lp, LANES), lambda b: (b, 0, 0)),
            pl.BlockSpec((tk, tk), lambda b: (0, 0)),
        ],
        out_specs=[
            pl.BlockSpec((None, lp, LANES), lambda b: (b, 0, 0)),
            pl.BlockSpec((None, nk, N_HEADS, tk), lambda b: (b, 0, 0, 0)),
        ],
        out_shape=[
            jax.ShapeDtypeStruct((bsz, lp, LANES), F32),
            jax.ShapeDtypeStruct((bsz, nk, N_HEADS, tk), F32),
        ],
        compiler_params=_cparams("parallel"),
        name="forget_cumsum",
    )(lf, tri)


def _attn_kernel(q_ref, k_ref, v_ref, cc_ref, cr_ref, g_ref, o_ref):
    tq = q_ref.shape[0]
    tk = cr_ref.shape[2]
    i = pl.program_id(1)
    lane = lax.broadcasted_iota(I32, (1, LANES), 1)
    lo = lane < HEAD_DIM
    row = lax.broadcasted_iota(I32, (tq, tk), 0)
    col = lax.broadcasted_iota(I32, (tq, tk), 1)
    causal = col <= row
    zero_b = jnp.zeros((), BF16)
    outs = []
    for p in range(N_HEADS // 2):
        cols = slice(p * LANES, (p + 1) * LANES)
        qp = q_ref[:, cols]
        qh = (jnp.where(lo, qp, zero_b), jnp.where(lo, zero_b, qp))
        cq = (cc_ref[:, 2 * p:2 * p + 1], cc_ref[:, 2 * p + 1:2 * p + 2])

        def block(j, carry, masked, cols=cols, qh=qh, cq=cq, p=p):
            start = pl.multiple_of(j * tk, tk)
            kb = k_ref[pl.ds(start, tk), cols]
            vb = v_ref[pl.ds(start, tk), cols]
            vh = (jnp.where(lo, vb, zero_b), jnp.where(lo, zero_b, vb))
            acc = carry[4]
            new = []
            alphas = []
            pv = None
            for hh in range(2):
                m_old, l_old = carry[2 * hh], carry[2 * hh + 1]
                s = lax.dot_general(qh[hh], kb, _NT, preferred_element_type=F32)
                s = s + (cq[hh] - cr_ref[j, 2 * p + hh:2 * p + hh + 1, :])
                if masked:
                    s = jnp.where(causal, s, NEG_BIG)
                m_new = jnp.maximum(m_old, jnp.max(s, axis=-1, keepdims=True))
                pr = jnp.exp(s - m_new)
                alpha = jnp.exp(m_old - m_new)
                l_new = alpha * l_old + jnp.sum(pr, axis=-1, keepdims=True)
                d = jnp.dot(pr.astype(BF16), vh[hh], preferred_element_type=F32)
                pv = d if pv is None else pv + d
                new += [m_new, l_new]
                alphas.append(alpha)
            acc = acc * jnp.where(lo, alphas[0], alphas[1]) + pv
            return (*new, acc)

        init = (jnp.full((tq, 1), NEG_BIG, F32), jnp.zeros((tq, 1), F32),
                jnp.full((tq, 1), NEG_BIG, F32), jnp.zeros((tq, 1), F32),
                jnp.zeros((tq, LANES), F32))
        carry = lax.fori_loop(0, i, functools.partial(block, masked=False), init)
        carry = block(i, carry, True)
        outs.append(carry[4] / jnp.where(lo, carry[1], carry[3]))
    y = jnp.concatenate(outs, axis=1)
    o_ref[...] = _rms(y, g_ref[...]).astype(o_ref.dtype)


def _attention(q, k, v, cc, cr, g, tq):
    bsz, lp, _ = q.shape
    nk, _, tk = cr.shape[1:]
    assert tq == tk
    return pl.pallas_call(
        _attn_kernel,
        grid=(bsz, lp // tq),
        in_specs=[
            pl.BlockSpec((None, tq, D_ATTN), lambda b, i: (b, i, 0)),
            pl.BlockSpec((None, lp, D_ATTN), lambda b, i: (b, 0, 0)),
            pl.BlockSpec((None, lp, D_ATTN), lambda b, i: (b, 0, 0)),
            pl.BlockSpec((None, tq, LANES), lambda b, i: (b, i, 0)),
            pl.BlockSpec((None, nk, N_HEADS, tk), lambda b, i: (b, 0, 0, 0)),
            pl.BlockSpec((1, D_ATTN), lambda b, i: (0, 0)),
        ],
        out_specs=pl.BlockSpec((None, tq, D_ATTN), lambda b, i: (b, i, 0)),
        out_shape=jax.ShapeDtypeStruct((bsz, lp, D_ATTN), BF16),
        compiler_params=_cparams("parallel", "parallel"),
        name="fox_attention",
    )(q, k, v, cc, cr, g)


def _mix_kernel(h_ref, ys_ref, ya_ref, wglu_ref, bglu_ref, gs_ref, wout_ref, gf_ref,
                wrh_ref, wrl_ref, br_ref, tri_ref,
                ho_ref, xn_ref, idx_ref, wts_ref, rnk_ref, cnt_ref, carry_ref):
    i = pl.program_id(0)
    tm = h_ref.shape[0]

    @pl.when(i == 0)
    def _():
        carry_ref[...] = jnp.zeros_like(carry_ref)

    ysm = jnp.concatenate([ys_ref[gb].astype(F32) for gb in range(N_BUNDLES)], axis=1)
    c0 = math.sqrt(2.0 / math.pi)
    gl = 0.5 * ysm * (1.0 + jnp.tanh(c0 * (ysm + 0.044715 * (ysm * ysm * ysm))))
    z = jnp.dot(gl.astype(BF16), wglu_ref[...], preferred_element_type=F32) + bglu_ref[...]
    y2 = gl * _sigmoid(z)
    mixed = jnp.concatenate([_rms(y2, gs_ref[...]).astype(BF16), ya_ref[...]], axis=1)
    hn = h_ref[...] + jnp.dot(mixed, wout_ref[...], preferred_element_type=F32)
    ho_ref[...] = hn
    xn = _rms(hn, gf_ref[...])
    xn_ref[...] = xn

    xh = xn.astype(BF16)
    xl = (xn - xh.astype(F32)).astype(BF16)
    wrh = wrh_ref[...]
    lg = (lax.dot_general(wrh, xh, _NT, preferred_element_type=F32)
          + lax.dot_general(wrh, xl, _NT, preferred_element_type=F32)
          + lax.dot_general(wrl_ref[...], xh, _NT, preferred_element_type=F32)) + br_ref[...]

    e_iota = lax.broadcasted_iota(I32, (N_EXPERTS, tm), 0)
    work = lg
    sel = jnp.zeros((N_EXPERTS, tm), jnp.bool_)
    top_i, top_v = [], []
    for _ in range(TOP_K):
        m = jnp.max(work, axis=0, keepdims=True)
        ik = jnp.min(jnp.where(work == m, e_iota, N_EXPERTS), axis=0, keepdims=True)
        hit = e_iota == ik
        sel = jnp.logical_or(sel, hit)
        work = jnp.where(hit, -jnp.inf, work)
        top_i.append(ik)
        top_v.append(m)
    tv = jnp.concatenate(top_v, axis=0)
    ew = jnp.exp(tv - tv[0:1])
    wts_ref[...] = ew / jnp.sum(ew, axis=0, keepdims=True)
    idx_ref[...] = jnp.concatenate(top_i, axis=0)

    onehot = sel.astype(BF16)
    rank = jnp.dot(onehot, tri_ref[...], preferred_element_type=F32) + carry_ref[:, 0:1]
    rnk_ref[...] = jnp.concatenate(
        [jnp.sum(jnp.where(e_iota == ik, rank, 0.0), axis=0, keepdims=True) for ik in top_i],
        axis=0).astype(I32)
    carry_ref[...] = carry_ref[...] + jnp.sum(sel.astype(F32), axis=1, keepdims=True)
    cnt_ref[...] = carry_ref[...]


def _mix_route(h, ys, ya, wglu, bglu, gs, wout, gf, wrh, wrl, br, tm):
    tp, d = h.shape
    tri = (jnp.arange(tm)[:, None] < jnp.arange(tm)[None, :]).astype(BF16)
    full = lambda shape: pl.BlockSpec(shape, lambda i: (0,) * len(shape))
    return pl.pallas_call(
        _mix_kernel,
        grid=(tp // tm,),
        in_specs=[
            pl.BlockSpec((tm, d), lambda i: (i, 0)),
            pl.BlockSpec((N_BUNDLES, tm, LANES), lambda i: (0, i, 0)),
            pl.BlockSpec((tm, D_ATTN), lambda i: (i, 0)),
            full((D_SSM, D_SSM)), full((1, D_SSM)), full((1, D_SSM)),
            full((d, d)), full((1, d)),
            full((N_EXPERTS, d)), full((N_EXPERTS, d)), full((N_EXPERTS, 1)),
            full((tm, tm)),
        ],
        out_specs=[
            pl.BlockSpec((tm, d), lambda i: (i, 0)),
            pl.BlockSpec((tm, d), lambda i: (i, 0)),
            pl.BlockSpec((TOP_K, tm), lambda i: (0, i)),
            pl.BlockSpec((TOP_K, tm), lambda i: (0, i)),
            pl.BlockSpec((TOP_K, tm), lambda i: (0, i)),
            pl.BlockSpec((N_EXPERTS, LANES), lambda i: (0, 0)),
        ],
        out_shape=[
            jax.ShapeDtypeStruct((tp, d), F32),
            jax.ShapeDtypeStruct((tp, d), F32),
            jax.ShapeDtypeStruct((TOP_K, tp), I32),
            jax.ShapeDtypeStruct((TOP_K, tp), F32),
            jax.ShapeDtypeStruct((TOP_K, tp), I32),
            jax.ShapeDtypeStruct((N_EXPERTS, LANES), F32),
        ],
        scratch_shapes=[pltpu.VMEM((N_EXPERTS, LANES), F32)],
        compiler_params=_cparams("arbitrary"),
        name="mix_route",
    )(h, ys, ya, wglu, bglu, gs, wout, gf, wrh, wrl, br, tri)


def _row_copy(src_ref, src_row, dst_ref, dst_row, sem):
    return pltpu.make_async_copy(src_ref.at[pl.ds(src_row, 1), :], dst_ref.at[pl.ds(dst_row, 1), :], sem)


def _dispatch_kernel(slot_ref, xn_ref, xs_ref, sem):
    tm = xn_ref.shape[0]

    def issue(t, c):
        for k in range(TOP_K):
            _row_copy(xn_ref, t, xs_ref, slot_ref[0, k * tm + t], sem).start()
        return c

    lax.fori_loop(0, tm, issue, 0)

    def drain(t, c):
        for k in range(TOP_K):
            _row_copy(xn_ref, 0, xs_ref, 0, sem).wait()
        return c

    lax.fori_loop(0, tm, drain, 0)


def _dispatch(slots, xn, n_slots, tm):
    tp, d = xn.shape
    return pl.pallas_call(
        _dispatch_kernel,
        grid=(tp // tm,),
        in_specs=[
            pl.BlockSpec((None, 1, TOP_K * tm), lambda i: (i, 0, 0), memory_space=pltpu.SMEM),
            pl.BlockSpec((tm, d), lambda i: (i, 0)),
        ],
        out_specs=pl.BlockSpec(memory_space=pl.ANY),
        out_shape=jax.ShapeDtypeStruct((n_slots, d), F32),
        scratch_shapes=[pltpu.SemaphoreType.DMA],
        compiler_params=_cparams("arbitrary"),
        name="moe_dispatch",
    )(slots, xn)


def _expert_kernel(wt_ref, we_ref, lo_ref, hi_ref, xs_ref, wg_ref, wu_ref, wd_ref, bg_ref, bu_ref, bd_ref,
                   ys_ref, wgb_ref, wub_ref, wdb_ref):
    w = pl.program_id(0)
    prev = jnp.maximum(w - 1, 0)
    lo, hi = lo_ref[w], hi_ref[w]
    valid = hi > lo
    new_expert = jnp.logical_or(w == 0, we_ref[w] != we_ref[prev])
    new_tile = jnp.logical_or(w == 0, wt_ref[w] != wt_ref[prev])

    @pl.when(jnp.logical_and(valid, new_expert))
    def _():
        wgb_ref[...] = wg_ref[...].astype(BF16)
        wub_ref[...] = wu_ref[...].astype(BF16)
        wdb_ref[...] = wd_ref[...].astype(BF16)

    def rows_of_item():
        x = xs_ref[...].astype(BF16)
        g = jnp.minimum(jnp.dot(x, wgb_ref[...], preferred_element_type=F32) + bg_ref[...], SWIGLU_LIMIT)
        u = jnp.clip(jnp.dot(x, wub_ref[...], preferred_element_type=F32) + bu_ref[...],
                     -SWIGLU_LIMIT, SWIGLU_LIMIT)
        hm = (g * _sigmoid(SWIGLU_ALPHA * g) * (u + 1.0)).astype(BF16)
        y = jnp.dot(hm, wdb_ref[...], preferred_element_type=F32) + bd_ref[...]
        row = lax.broadcasted_iota(I32, (EXPERT_TILE, 1), 0)
        return y, jnp.logical_and(row >= lo, row < hi)

    @pl.when(jnp.logical_and(valid, new_tile))
    def _():
        y, mine = rows_of_item()
        ys_ref[...] = jnp.where(mine, y, 0.0)

    @pl.when(jnp.logical_and(valid, jnp.logical_not(new_tile)))
    def _():
        y, mine = rows_of_item()
        ys_ref[...] = jnp.where(mine, y, ys_ref[...])


def _experts(item_tile, item_expert, item_lo, item_hi, xs, wg, wu, wd, bg, bu, bd):
    n_slots, d = xs.shape
    dff = wg.shape[2]
    wspec = lambda r, c: pl.BlockSpec((None, r, c), lambda w, wt, we, lo, hi: (we[w], 0, 0))
    tspec = pl.BlockSpec((EXPERT_TILE, d), lambda w, wt, we, lo, hi: (wt[w], 0))
    return pl.pallas_call(
        _expert_kernel,
        grid_spec=pltpu.PrefetchScalarGridSpec(
            num_scalar_prefetch=4,
            grid=(item_tile.shape[0],),
            in_specs=[
                tspec,
                wspec(d, dff), wspec(d, dff), wspec(dff, d),
                wspec(1, dff), wspec(1, dff), wspec(1, d),
            ],
            out_specs=tspec,
            scratch_shapes=[pltpu.VMEM((d, dff), BF16), pltpu.VMEM((d, dff), BF16), pltpu.VMEM((dff, d), BF16)],
        ),
        out_shape=jax.ShapeDtypeStruct((n_slots, d), F32),
        compiler_params=_cparams("arbitrary"),
        name="moe_experts",
    )(item_tile, item_expert, item_lo, item_hi, xs, wg, wu, wd, bg, bu, bd)


def _combine_kernel(slot_ref, h_ref, w_ref, g_ref, ys_ref, o_ref, buf_ref, sem, *, final_norm):
    tm = h_ref.shape[0]

    def issue(t, c):
        for k in range(TOP_K):
            _row_copy(ys_ref, slot_ref[0, k * tm + t], buf_ref.at[k], t, sem).start()
        return c

    lax.fori_loop(0, tm, issue, 0)

    def drain(t, c):
        for k in range(TOP_K):
            _row_copy(ys_ref, 0, buf_ref.at[k], 0, sem).wait()
        return c

    lax.fori_loop(0, tm, drain, 0)
    out = h_ref[...]
    for k in range(TOP_K):
        out = out + w_ref[:, k:k + 1] * buf_ref[k]
    if final_norm:
        out = _rms(out, g_ref[...])
    o_ref[...] = out


def _combine(slots, h, wcol, g, ys, tm, final_norm):
    tp, d = h.shape
    return pl.pallas_call(
        functools.partial(_combine_kernel, final_norm=final_norm),
        grid=(tp // tm,),
        in_specs=[
            pl.BlockSpec((None, 1, TOP_K * tm), lambda i: (i, 0, 0), memory_space=pltpu.SMEM),
            pl.BlockSpec((tm, d), lambda i: (i, 0)),
            pl.BlockSpec((tm, TOP_K), lambda i: (i, 0)),
            pl.BlockSpec((1, d), lambda i: (0, 0)),
            pl.BlockSpec(memory_space=pl.ANY),
        ],
        out_specs=pl.BlockSpec((tm, d), lambda i: (i, 0)),
        out_shape=jax.ShapeDtypeStruct((tp, d), F32),
        scratch_shapes=[pltpu.VMEM((TOP_K, tm, d), F32), pltpu.SemaphoreType.DMA],
        compiler_params=_cparams("arbitrary"),
        name="moe_combine",
    )(slots, h, wcol, g, ys)


def _routing_tables(cnt, idx, rnk, n_items, tm):
    cnt = cnt.astype(I32)
    end = jnp.cumsum(cnt)
    offs = end - cnt
    slot = offs[idx] + rnk
    first_tile = offs // EXPERT_TILE
    n_e = jnp.where(cnt > 0, (end - 1) // EXPERT_TILE - first_tile + 1, 0)
    cum_items = jnp.cumsum(n_e)
    w = jnp.minimum(jnp.arange(n_items, dtype=I32), cum_items[-1] - 1)
    e_w = jnp.minimum(jnp.searchsorted(cum_items, w, side="right").astype(I32), N_EXPERTS - 1)
    tile_w = first_tile[e_w] + (w - (cum_items - n_e)[e_w])
    base = tile_w * EXPERT_TILE
    lo_w = jnp.maximum(offs[e_w], base) - base
    hi_w = jnp.minimum(end[e_w], base + EXPERT_TILE) - base
    hi_w = jnp.where(jnp.arange(n_items) < cum_items[-1], hi_w, lo_w)
    tp = slot.shape[1]
    slots = slot.reshape(TOP_K, tp // tm, tm).transpose(1, 0, 2).reshape(tp // tm, 1, TOP_K * tm)
    return slots, tile_w.astype(I32), e_w, lo_w.astype(I32), hi_w.astype(I32)


def kernel(x, meta_tokens, norm_mix, w_in, ssm_lambda_re, ssm_lambda_im, ssm_log_dt, ssm_b_re, ssm_b_im, ssm_c_re, ssm_c_im, ssm_d, w_glu, b_glu, b_forget, norm_ssm_out, norm_attn_out, w_out, norm_ffn, w_router, b_router, w_gate, b_gate, w_up, b_up, w_down, b_down, norm_final):
    bsz, seq, d = x.shape
    depth = w_in.shape[0]
    l_real = N_META + seq
    lp = -(-l_real // SEQ_ALIGN) * SEQ_ALIGN
    tp = bsz * lp
    nc = lp // SSM_CHUNK
    tm_proj = 512 if tp % 512 == 0 else 256
    tm_moe = 256
    n_slots = TOP_K * tp
    n_items = n_slots // EXPERT_TILE + N_EXPERTS
    o_f = D_SSM + 3 * D_ATTN

    meta = jnp.broadcast_to(meta_tokens[None].astype(x.dtype), (bsz, N_META, d))
    pad = jnp.zeros((bsz, lp - l_real, d), x.dtype)
    h = jnp.concatenate([meta, x, pad], axis=1).reshape(tp, d)

    for l in range(depth):
        w_main = jnp.pad(w_in[l], ((0, 0), (0, LANES - N_HEADS))).astype(BF16)
        bf = jnp.pad(b_forget[l].astype(F32), (0, LANES - N_HEADS)).reshape(1, LANES)
        u, q, k, v, lf = _in_proj(h, norm_mix[l].reshape(1, d), w_main, bf, tm_proj)

        wt, wb, wc, a16 = _ssm_operators(ssm_lambda_re[l], ssm_lambda_im[l], ssm_log_dt[l], ssm_b_re[l],
                                         ssm_b_im[l], ssm_c_re[l], ssm_c_im[l], ssm_d[l])
        ys = _ssm(u.reshape(N_BUNDLES, bsz, nc, BUNDLE_IN), wt, wb, wc, a16)
        ys = ys.reshape(N_BUNDLES, tp, LANES)

        cc, cr = _forget_cumsum(lf.reshape(bsz, lp, LANES), SEQ_ALIGN)
        ya = _attention(q.reshape(bsz, lp, D_ATTN), k.reshape(bsz, lp, D_ATTN), v.reshape(bsz, lp, D_ATTN),
                        cc, cr, norm_attn_out[l].reshape(1, D_ATTN), SEQ_ALIGN).reshape(tp, D_ATTN)

        wr = w_router[l].T.astype(F32)
        wrh = wr.astype(BF16)
        wrl = (wr - wrh.astype(F32)).astype(BF16)
        h, xn, idx, wts, rnk, cnt = _mix_route(
            h, ys, ya, w_glu[l].astype(BF16), b_glu[l].reshape(1, D_SSM), norm_ssm_out[l].reshape(1, D_SSM),
            w_out[l].astype(BF16), norm_ffn[l].reshape(1, d), wrh, wrl,
            b_router[l].astype(F32).reshape(N_EXPERTS, 1), tm_proj)

        slots, item_tile, item_expert, item_lo, item_hi = _routing_tables(cnt[:, 0], idx, rnk, n_items, tm_moe)
        xs = _dispatch(slots, xn, n_slots, tm_moe)
        yo = _experts(item_tile, item_expert, item_lo, item_hi, xs, w_gate[l], w_up[l], w_down[l],
                      b_gate[l].reshape(N_EXPERTS, 1, -1), b_up[l].reshape(N_EXPERTS, 1, -1),
                      b_down[l].reshape(N_EXPERTS, 1, -1))
        h = _combine(slots, h, wts.T, norm_final.reshape(1, d), yo, tm_moe, l == depth - 1)

    return h.reshape(bsz, lp, d)[:, N_META:l_real]
```

```python
import functools
import math

import jax
import jax.numpy as jnp
from jax import lax
from jax.experimental import pallas as pl
from jax.experimental.pallas import tpu as pltpu

F32 = jnp.float32
BF16 = jnp.bfloat16
I32 = jnp.int32

N_META = 16
HEAD_DIM = 64
N_HEADS = 8
D_ATTN = N_HEADS * HEAD_DIM
D_SSM = 512
SSM_GROUP = 16
SSM_STATE = 64
N_GROUPS = D_SSM // SSM_GROUP
N_EXPERTS = 32
TOP_K = 4
SWIGLU_LIMIT = 7.0
SWIGLU_ALPHA = 1.702
RMS_EPS = 1e-6

LANES = 128
SEQ_ALIGN = 256
SSM_CHUNK = 16
GROUPS_PER_BUNDLE = LANES // SSM_GROUP
N_BUNDLES = N_GROUPS // GROUPS_PER_BUNDLE
BUNDLE_IN = SSM_CHUNK * LANES
BUNDLE_STATE = 2 * GROUPS_PER_BUNDLE * SSM_STATE
EXPERT_TILE = 256
VMEM_LIMIT = 56 * 1024 * 1024
NEG_BIG = -1e30

_NT = (((1,), (1,)), ((), ()))


def _cparams(*sem):
    return pltpu.CompilerParams(dimension_semantics=sem, vmem_limit_bytes=VMEM_LIMIT)


def _rms(x, g):
    return x * lax.rsqrt(jnp.mean(x * x, axis=-1, keepdims=True) + RMS_EPS) * g


def _sigmoid(x):
    return 1.0 / (1.0 + jnp.exp(-x))


N_SPLIT = 3
AUG_ROWS = LANES - HEAD_DIM


def _split3(x):
    x1 = x.astype(BF16)
    r1 = x - x1.astype(F32)
    x2 = r1.astype(BF16)
    x3 = (r1 - x2.astype(F32)).astype(BF16)
    return x1, x2, x3


def _in_proj_kernel(h_ref, g_ref, wa_ref, wbt_ref, bf_ref, tri_ref, place_ref,
                    u_ref, ka_ref, qa_ref, vt_ref, carry_ref):
    j = pl.program_id(1)
    tm = h_ref.shape[0]

    @pl.when(j == 0)
    def _():
        carry_ref[...] = jnp.zeros_like(carry_ref)

    xb = _rms(h_ref[...], g_ref[...]).astype(BF16)
    pa = jnp.dot(xb, wa_ref[...], preferred_element_type=F32)
    for gb in range(N_BUNDLES):
        u_ref[gb] = pa[:, gb * LANES:(gb + 1) * LANES].astype(BF16)
    o_k = D_SSM
    o_f = D_SSM + N_HEADS * LANES

    z = pa[:, o_f:o_f + LANES] + bf_ref[...]
    lf = jnp.minimum(z, 0.0) - jnp.log(1.0 + jnp.exp(-jnp.abs(z)))
    tri = tri_ref[...]
    cs = carry_ref[...]
    for piece in _split3(lf):
        cs = cs + jnp.dot(tri, piece, preferred_element_type=F32)
    carry_ref[...] = cs[tm - 1:tm, :]
    aug = jnp.dot(jnp.concatenate(_split3(cs), axis=1), place_ref[...], preferred_element_type=F32)
    ka_ref[...] = (pa[:, o_k:o_f] + aug).astype(BF16)

    pbt = lax.dot_general(wbt_ref[...], xb, _NT, preferred_element_type=F32)
    row = lax.broadcasted_iota(I32, (AUG_ROWS, tm), 0)
    ones_rows = jnp.where(row < N_SPLIT, 1.0, 0.0).astype(BF16)
    for hd in range(N_HEADS):
        qa_ref[hd * LANES:hd * LANES + HEAD_DIM, :] = (
            pbt[hd * HEAD_DIM:(hd + 1) * HEAD_DIM, :] * (HEAD_DIM ** -0.5)).astype(BF16)
        qa_ref[hd * LANES + HEAD_DIM:(hd + 1) * LANES, :] = ones_rows
    tk = vt_ref.shape[2]
    for blk in range(tm // tk):
        vt_ref[blk] = pbt[D_ATTN:, blk * tk:(blk + 1) * tk].astype(BF16)


def _in_proj(h, g, wa, wbt, bf, tm, tk):
    bsz, lp, d = h.shape
    tri = (jnp.arange(tm)[:, None] >= jnp.arange(tm)[None, :]).astype(BF16)
    src = jnp.arange(N_SPLIT * LANES)
    dst = (src % LANES) * LANES + HEAD_DIM + src // LANES
    place = jnp.where((src % LANES < N_HEADS)[:, None] & (dst[:, None] == jnp.arange(N_HEADS * LANES)[None, :]),
                      -1.0, 0.0).astype(BF16)
    full = lambda a: pl.BlockSpec(a.shape, lambda b, j: (0,) * a.ndim)
    return pl.pallas_call(
        _in_proj_kernel,
        grid=(bsz, lp // tm),
        in_specs=[
            pl.BlockSpec((None, tm, d), lambda b, j: (b, j, 0)),
            full(g), full(wa), full(wbt), full(bf), full(tri), full(place),
        ],
        out_specs=[
            pl.BlockSpec((N_BUNDLES, None, tm, LANES), lambda b, j: (0, b, j, 0)),
            pl.BlockSpec((None, tm, N_HEADS * LANES), lambda b, j: (b, j, 0)),
            pl.BlockSpec((None, N_HEADS * LANES, tm), lambda b, j: (b, 0, j)),
            pl.BlockSpec((None, tm // tk, D_ATTN, tk), lambda b, j: (b, j, 0, 0)),
        ],
        out_shape=[
            jax.ShapeDtypeStruct((N_BUNDLES, bsz, lp, LANES), BF16),
            jax.ShapeDtypeStruct((bsz, lp, N_HEADS * LANES), BF16),
            jax.ShapeDtypeStruct((bsz, N_HEADS * LANES, lp), BF16),
            jax.ShapeDtypeStruct((bsz, lp // tk, D_ATTN, tk), BF16),
        ],
        scratch_shapes=[pltpu.VMEM((1, LANES), F32)],
        compiler_params=_cparams("parallel", "arbitrary"),
        name="in_proj",
    )(h, g, wa, wbt, bf, tri, place)


def _ssm_kernel(x_ref, wt_ref, wb_ref, wc_ref, a_ref, o_ref, z_ref, s_ref):
    x = x_ref[...]
    nc = x.shape[0]
    half = BUNDLE_STATE // 2
    z_ref[...] = jnp.dot(x, wb_ref[...], preferred_element_type=F32)
    a_re = a_ref[:, :half]
    a_im = a_ref[:, half:]

    def step(c8, carry):
        s_re, s_im = carry
        start = pl.multiple_of(c8 * 8, 8)
        zblk = z_ref[pl.ds(start, 8), :]
        rows = []
        for r in range(8):
            rows.append(jnp.concatenate([s_re, s_im], axis=1))
            z_re = zblk[r:r + 1, :half]
            z_im = zblk[r:r + 1, half:]
            s_re, s_im = (a_re * s_re - a_im * s_im + z_re,
                          a_re * s_im + a_im * s_re + z_im)
        s_ref[pl.ds(start, 8), :] = jnp.concatenate(rows, axis=0)
        return s_re, s_im

    zero = jnp.zeros((1, half), F32)
    lax.fori_loop(0, nc // 8, step, (zero, zero))
    y = jnp.dot(x, wt_ref[...], preferred_element_type=F32)
    y = y + jnp.dot(s_ref[...].astype(BF16), wc_ref[...], preferred_element_type=F32)
    o_ref[...] = y.astype(o_ref.dtype)


def _ssm(xb, wt, wb, wc, a16):
    nb, bsz, nc, _ = xb.shape
    return pl.pallas_call(
        _ssm_kernel,
        grid=(nb, bsz),
        in_specs=[
            pl.BlockSpec((None, None, nc, BUNDLE_IN), lambda g, b: (g, b, 0, 0)),
            pl.BlockSpec((None, BUNDLE_IN, BUNDLE_IN), lambda g, b: (g, 0, 0)),
            pl.BlockSpec((None, BUNDLE_IN, BUNDLE_STATE), lambda g, b: (g, 0, 0)),
            pl.BlockSpec((None, BUNDLE_STATE, BUNDLE_IN), lambda g, b: (g, 0, 0)),
            pl.BlockSpec((None, 1, BUNDLE_STATE), lambda g, b: (g, 0, 0)),
        ],
        out_specs=pl.BlockSpec((None, None, nc, BUNDLE_IN), lambda g, b: (g, b, 0, 0)),
        out_shape=jax.ShapeDtypeStruct((nb, bsz, nc, BUNDLE_IN), BF16),
        scratch_shapes=[pltpu.VMEM((nc, BUNDLE_STATE), F32), pltpu.VMEM((nc, BUNDLE_STATE), F32)],
        compiler_params=_cparams("parallel", "parallel"),
        name="s5_chunked",
    )(xb, wt, wb, wc, a16)


def _ssm_operators(lam_re, lam_im, log_dt, b_re, b_im, c_re, c_im, d):
    g_, p_, h_, q_ = N_GROUPS, SSM_STATE, SSM_GROUP, SSM_CHUNK
    nb, g8 = N_BUNDLES, GROUPS_PER_BUNDLE
    hi = lax.Precision.HIGHEST
    lr, li = lam_re.astype(F32), lam_im.astype(F32)
    dt = jnp.exp(log_dt.astype(F32))[:, None]
    steps = jnp.arange(q_ + 1, dtype=F32)[:, None, None]
    mag = jnp.exp((lr * dt)[None] * steps)
    ang = (li * dt)[None] * steps
    pw_re, pw_im = mag * jnp.cos(ang), mag * jnp.sin(ang)
    n_re, n_im = pw_re[1] - 1.0, pw_im[1]
    den = lr * lr + li * li
    z_re, z_im = (n_re * lr + n_im * li) / den, (n_im * lr - n_re * li) / den
    br, bi = b_re.astype(F32), b_im.astype(F32)
    bb_re = z_re[:, :, None] * br - z_im[:, :, None] * bi
    bb_im = z_re[:, :, None] * bi + z_im[:, :, None] * br
    cr, ci = c_re.astype(F32), c_im.astype(F32)

    def blockdiag(x):
        r, c = x.shape[-2:]
        tile = jnp.tile(jnp.eye(c, dtype=F32), (1, g8))
        wide = jnp.einsum("...rc,cy->...ry", x.reshape(*x.shape[:-3], g8 * r, c), tile, precision=hi)
        same = (jnp.arange(g8 * r)[:, None] // r) == (jnp.arange(g8 * c)[None, :] // c)
        return jnp.where(same, wide, 0.0)

    cl_re = cr[None] * pw_re[:q_, :, None, :] - ci[None] * pw_im[:q_, :, None, :]
    cl_im = cr[None] * pw_im[:q_, :, None, :] + ci[None] * pw_re[:q_, :, None, :]
    kd = (jnp.einsum("kgep,gph->kghe", cl_re, bb_re, precision=hi)
          - jnp.einsum("kgep,gph->kghe", cl_im, bb_im, precision=hi))
    skip = jnp.eye(h_, dtype=F32)[None] * d.astype(F32).reshape(g_, h_, 1)
    kd = kd.at[0].add(skip)
    blocks = blockdiag(kd.reshape(q_, nb, g8, h_, h_)).astype(BF16)
    zero = jnp.zeros_like(blocks[:1])
    wt = jnp.stack([jnp.concatenate([zero] * t + [blocks[:q_ - t]], axis=0) for t in range(q_)])
    wt = wt.transpose(2, 0, 3, 1, 4).reshape(nb, BUNDLE_IN, BUNDLE_IN)

    rev_re, rev_im = pw_re[q_ - 1::-1][:q_], pw_im[q_ - 1::-1][:q_]
    bp_re = rev_re[:, :, :, None] * bb_re[None] - rev_im[:, :, :, None] * bb_im[None]
    bp_im = rev_re[:, :, :, None] * bb_im[None] + rev_im[:, :, :, None] * bb_re[None]
    to_hp = lambda x: x.transpose(0, 1, 3, 2).reshape(q_, nb, g8, h_, p_)
    wb = jnp.concatenate([blockdiag(to_hp(bp_re)), blockdiag(to_hp(bp_im))], axis=-1).astype(BF16)
    wb = wb.transpose(1, 0, 2, 3).reshape(nb, BUNDLE_IN, BUNDLE_STATE)

    cq_re = cr[None] * pw_re[1:, :, None, :] - ci[None] * pw_im[1:, :, None, :]
    cq_im = cr[None] * pw_im[1:, :, None, :] + ci[None] * pw_re[1:, :, None, :]
    to_ph = lambda x: x.transpose(0, 1, 3, 2).reshape(q_, nb, g8, p_, h_)
    wc = jnp.concatenate([blockdiag(to_ph(cq_re)), blockdiag(to_ph(-cq_im))], axis=-2).astype(BF16)
    wc = wc.transpose(1, 2, 0, 3).reshape(nb, BUNDLE_STATE, BUNDLE_IN)

    a16 = jnp.stack([pw_re[q_], pw_im[q_]]).reshape(2, nb, g8 * p_)
    a16 = a16.transpose(1, 0, 2).reshape(nb, 1, BUNDLE_STATE)
    return wt, wb, wc, a16


def _attn_kernel(qa_ref, ka_ref, vt_ref, g_ref, o_ref):
    tq = qa_ref.shape[1]
    tk = vt_ref.shape[2]
    i = pl.program_id(1)
    causal = lax.broadcasted_iota(I32, (tk, tq), 0) <= lax.broadcasted_iota(I32, (tk, tq), 1)

    def block(j, carry, masked):
        start = pl.multiple_of(j * tk, tk)
        scores = []
        for hd in range(N_HEADS):
            kb = ka_ref[pl.ds(start, tk), hd * LANES:(hd + 1) * LANES]
            scores.append(jnp.dot(kb, qa_ref[hd * LANES:(hd + 1) * LANES, :], preferred_element_type=F32))
        stats = []
        for hd in range(N_HEADS):
            m_old, l_old = carry[3 * hd:3 * hd + 2]
            s = scores[hd]
            if masked:
                s = jnp.where(causal, s, NEG_BIG)
            m_new = jnp.maximum(m_old, jnp.max(s, axis=0, keepdims=True))
            pr = jnp.exp(s - m_new)
            alpha = jnp.exp(m_old - m_new)
            l_new = alpha * l_old + jnp.sum(pr, axis=0, keepdims=True)
            stats.append((m_new, l_new, alpha, pr.astype(BF16)))
        new = []
        for hd in range(N_HEADS):
            m_new, l_new, alpha, pb = stats[hd]
            vb = vt_ref[j, hd * HEAD_DIM:(hd + 1) * HEAD_DIM, :]
            acc = alpha * carry[3 * hd + 2] + jnp.dot(vb, pb, preferred_element_type=F32)
            new += [m_new, l_new, acc]
        return tuple(new)

    init = (jnp.full((1, tq), NEG_BIG, F32), jnp.zeros((1, tq), F32), jnp.zeros((HEAD_DIM, tq), F32)) * N_HEADS
    carry = lax.fori_loop(0, i, functools.partial(block, masked=False), init)
    carry = block(i, carry, True)
    outs = [carry[3 * hd + 2] / carry[3 * hd + 1] for hd in range(N_HEADS)]
    y = jnp.concatenate(outs, axis=0).T
    o_ref[...] = _rms(y, g_ref[...]).astype(o_ref.dtype)


def _attention(qa, ka, vt, g, tq):
    bsz, lp, width = ka.shape
    nkv, _, tk = vt.shape[1:]
    assert tq == tk
    return pl.pallas_call(
        _attn_kernel,
        grid=(bsz, lp // tq),
        in_specs=[
            pl.BlockSpec((None, width, tq), lambda b, i: (b, 0, i)),
            pl.BlockSpec((None, lp, width), lambda b, i: (b, 0, 0)),
            pl.BlockSpec((None, nkv, D_ATTN, tk), lambda b, i: (b, 0, 0, 0)),
            pl.BlockSpec((1, D_ATTN), lambda b, i: (0, 0)),
        ],
        out_specs=pl.BlockSpec((None, tq, D_ATTN), lambda b, i: (b, i, 0)),
        out_shape=jax.ShapeDtypeStruct((bsz, lp, D_ATTN), BF16),
        compiler_params=_cparams("parallel", "parallel"),
        name="fox_attention",
    )(qa, ka, vt, g)


def _mix_kernel(h_ref, ys_ref, ya_ref, wglu_ref, bglu_ref, gs_ref, wout_ref, gf_ref,
                wrh_ref, wrl_ref, br_ref, tri_ref,
                ho_ref, xn_ref, idx_ref, wts_ref, rnk_ref, cnt_ref, carry_ref):
    i = pl.program_id(0)
    tm = h_ref.shape[0]

    @pl.when(i == 0)
    def _():
        carry_ref[...] = jnp.zeros_like(carry_ref)

    ysm = jnp.concatenate([ys_ref[gb].astype(F32) for gb in range(N_BUNDLES)], axis=1)
    c0 = math.sqrt(2.0 / math.pi)
    gl = 0.5 * ysm * (1.0 + jnp.tanh(c0 * (ysm + 0.044715 * (ysm * ysm * ysm))))
    z = jnp.dot(gl.astype(BF16), wglu_ref[...], preferred_element_type=F32) + bglu_ref[...]
    y2 = gl * _sigmoid(z)
    mixed = jnp.concatenate([_rms(y2, gs_ref[...]).astype(BF16), ya_ref[...]], axis=1)
    hn = h_ref[...] + jnp.dot(mixed, wout_ref[...], preferred_element_type=F32)
    ho_ref[...] = hn
    xn = _rms(hn, gf_ref[...])
    xn_ref[...] = xn

    xh = xn.astype(BF16)
    xl = (xn - xh.astype(F32)).astype(BF16)
    wrh = wrh_ref[...]
    lg = (lax.dot_general(wrh, xh, _NT, preferred_element_type=F32)
          + lax.dot_general(wrh, xl, _NT, preferred_element_type=F32)
          + lax.dot_general(wrl_ref[...], xh, _NT, preferred_element_type=F32)) + br_ref[...]

    e_iota = lax.broadcasted_iota(I32, (N_EXPERTS, tm), 0)
    work = lg
    sel = jnp.zeros((N_EXPERTS, tm), jnp.bool_)
    top_i, top_v = [], []
    for _ in range(TOP_K):
        m = jnp.max(work, axis=0, keepdims=True)
        ik = jnp.min(jnp.where(work == m, e_iota, N_EXPERTS), axis=0, keepdims=True)
        hit = e_iota == ik
        sel = jnp.logical_or(sel, hit)
        work = jnp.where(hit, -jnp.inf, work)
        top_i.append(ik)
        top_v.append(m)
    tv = jnp.concatenate(top_v, axis=0)
    ew = jnp.exp(tv - tv[0:1])
    wts_ref[...] = ew / jnp.sum(ew, axis=0, keepdims=True)
    idx_ref[...] = jnp.concatenate(top_i, axis=0)

    onehot = sel.astype(BF16)
    rank = jnp.dot(onehot, tri_ref[...], preferred_element_type=F32) + carry_ref[:, 0:1]
    rnk_ref[...] = jnp.concatenate(
        [jnp.sum(jnp.where(e_iota == ik, rank, 0.0), axis=0, keepdims=True) for ik in top_i],
        axis=0).astype(I32)
    carry_ref[...] = carry_ref[...] + jnp.sum(sel.astype(F32), axis=1, keepdims=True)
    cnt_ref[...] = carry_ref[...]


def _mix_route(h, ys, ya, wglu, bglu, gs, wout, gf, wrh, wrl, br, tm):
    tp, d = h.shape
    tri = (jnp.arange(tm)[:, None] < jnp.arange(tm)[None, :]).astype(BF16)
    full = lambda shape: pl.BlockSpec(shape, lambda i: (0,) * len(shape))
    return pl.pallas_call(
        _mix_kernel,
        grid=(tp // tm,),
        in_specs=[
            pl.BlockSpec((tm, d), lambda i: (i, 0)),
            pl.BlockSpec((N_BUNDLES, tm, LANES), lambda i: (0, i, 0)),
            pl.BlockSpec((tm, D_ATTN), lambda i: (i, 0)),
            full((D_SSM, D_SSM)), full((1, D_SSM)), full((1, D_SSM)),
            full((d, d)), full((1, d)),
            full((N_EXPERTS, d)), full((N_EXPERTS, d)), full((N_EXPERTS, 1)),
            full((tm, tm)),
        ],
        out_specs=[
            pl.BlockSpec((tm, d), lambda i: (i, 0)),
            pl.BlockSpec((tm, d), lambda i: (i, 0)),
            pl.BlockSpec((TOP_K, tm), lambda i: (0, i)),
            pl.BlockSpec((TOP_K, tm), lambda i: (0, i)),
            pl.BlockSpec((TOP_K, tm), lambda i: (0, i)),
            pl.BlockSpec((N_EXPERTS, LANES), lambda i: (0, 0)),
        ],
        out_shape=[
            jax.ShapeDtypeStruct((tp, d), F32),
            jax.ShapeDtypeStruct((tp, d), F32),
            jax.ShapeDtypeStruct((TOP_K, tp), I32),
            jax.ShapeDtypeStruct((TOP_K, tp), F32),
            jax.ShapeDtypeStruct((TOP_K, tp), I32),
            jax.ShapeDtypeStruct((N_EXPERTS, LANES), F32),
        ],
        scratch_shapes=[pltpu.VMEM((N_EXPERTS, LANES), F32)],
        compiler_params=_cparams("arbitrary"),
        name="mix_route",
    )(h, ys, ya, wglu, bglu, gs, wout, gf, wrh, wrl, br, tri)


def _row_copy(src_ref, src_row, dst_ref, dst_row, sem):
    return pltpu.make_async_copy(src_ref.at[pl.ds(src_row, 1), :], dst_ref.at[pl.ds(dst_row, 1), :], sem)


def _dispatch_kernel(slot_ref, xn_ref, xs_ref, sem):
    tm = xn_ref.shape[0]

    def issue(t, c):
        for k in range(TOP_K):
            _row_copy(xn_ref, t, xs_ref, slot_ref[0, k * tm + t], sem).start()
        return c

    lax.fori_loop(0, tm, issue, 0)

    def drain(t, c):
        for k in range(TOP_K):
            _row_copy(xn_ref, 0, xs_ref, 0, sem).wait()
        return c

    lax.fori_loop(0, tm, drain, 0)


def _dispatch(slots, xn, n_slots, tm):
    tp, d = xn.shape
    return pl.pallas_call(
        _dispatch_kernel,
        grid=(tp // tm,),
        in_specs=[
            pl.BlockSpec((None, 1, TOP_K * tm), lambda i: (i, 0, 0), memory_space=pltpu.SMEM),
            pl.BlockSpec((tm, d), lambda i: (i, 0)),
        ],
        out_specs=pl.BlockSpec(memory_space=pl.ANY),
        out_shape=jax.ShapeDtypeStruct((n_slots, d), F32),
        scratch_shapes=[pltpu.SemaphoreType.DMA],
        compiler_params=_cparams("arbitrary"),
        name="moe_dispatch",
    )(slots, xn)


def _expert_kernel(wt_ref, we_ref, lo_ref, hi_ref, xs_ref, wg_ref, wu_ref, wd_ref, bg_ref, bu_ref, bd_ref,
                   ys_ref, wgb_ref, wub_ref, wdb_ref):
    w = pl.program_id(0)
    prev = jnp.maximum(w - 1, 0)
    lo, hi = lo_ref[w], hi_ref[w]
    valid = hi > lo
    new_expert = jnp.logical_or(w == 0, we_ref[w] != we_ref[prev])
    new_tile = jnp.logical_or(w == 0, wt_ref[w] != wt_ref[prev])

    @pl.when(jnp.logical_and(valid, new_expert))
    def _():
        wgb_ref[...] = wg_ref[...].astype(BF16)
        wub_ref[...] = wu_ref[...].astype(BF16)
        wdb_ref[...] = wd_ref[...].astype(BF16)

    def rows_of_item():
        x = xs_ref[...].astype(BF16)
        g = jnp.minimum(jnp.dot(x, wgb_ref[...], preferred_element_type=F32) + bg_ref[...], SWIGLU_LIMIT)
        u = jnp.clip(jnp.dot(x, wub_ref[...], preferred_element_type=F32) + bu_ref[...],
                     -SWIGLU_LIMIT, SWIGLU_LIMIT)
        hm = (g * _sigmoid(SWIGLU_ALPHA * g) * (u + 1.0)).astype(BF16)
        y = jnp.dot(hm, wdb_ref[...], preferred_element_type=F32) + bd_ref[...]
        row = lax.broadcasted_iota(I32, (EXPERT_TILE, 1), 0)
        return y, jnp.logical_and(row >= lo, row < hi)

    @pl.when(jnp.logical_and(valid, new_tile))
    def _():
        y, mine = rows_of_item()
        ys_ref[...] = jnp.where(mine, y, 0.0)

    @pl.when(jnp.logical_and(valid, jnp.logical_not(new_tile)))
    def _():
        y, mine = rows_of_item()
        ys_ref[...] = jnp.where(mine, y, ys_ref[...])


def _experts(item_tile, item_expert, item_lo, item_hi, xs, wg, wu, wd, bg, bu, bd, layer):
    n_slots, d = xs.shape
    dff = wg.shape[3]
    wspec = lambda r, c: pl.BlockSpec((None, None, r, c), lambda w, wt, we, lo, hi: (layer, we[w], 0, 0))
    tspec = pl.BlockSpec((EXPERT_TILE, d), lambda w, wt, we, lo, hi: (wt[w], 0))
    return pl.pallas_call(
        _expert_kernel,
        grid_spec=pltpu.PrefetchScalarGridSpec(
            num_scalar_prefetch=4,
            grid=(item_tile.shape[0],),
            in_specs=[
                tspec,
                wspec(d, dff), wspec(d, dff), wspec(dff, d),
                wspec(1, dff), wspec(1, dff), wspec(1, d),
            ],
            out_specs=tspec,
            scratch_shapes=[pltpu.VMEM((d, dff), BF16), pltpu.VMEM((d, dff), BF16), pltpu.VMEM((dff, d), BF16)],
        ),
        out_shape=jax.ShapeDtypeStruct((n_slots, d), F32),
        compiler_params=_cparams("arbitrary"),
        name="moe_experts",
    )(item_tile, item_expert, item_lo, item_hi, xs, wg, wu, wd, bg, bu, bd)


def _combine_kernel(slot_ref, h_ref, w_ref, g_ref, ys_ref, o_ref, buf_ref, sem, *, final_norm):
    tm = h_ref.shape[0]

    def issue(t, c):
        for k in range(TOP_K):
            _row_copy(ys_ref, slot_ref[0, k * tm + t], buf_ref.at[k], t, sem).start()
        return c

    lax.fori_loop(0, tm, issue, 0)

    def drain(t, c):
        for k in range(TOP_K):
            _row_copy(ys_ref, 0, buf_ref.at[k], 0, sem).wait()
        return c

    lax.fori_loop(0, tm, drain, 0)
    out = h_ref[...]
    for k in range(TOP_K):
        out = out + w_ref[:, k:k + 1] * buf_ref[k]
    if final_norm:
        out = _rms(out, g_ref[...])
    o_ref[...] = out


def _combine(slots, h, wcol, g, ys, tm, final_norm):
    tp, d = h.shape
    return pl.pallas_call(
        functools.partial(_combine_kernel, final_norm=final_norm),
        grid=(tp // tm,),
        in_specs=[
            pl.BlockSpec((None, 1, TOP_K * tm), lambda i: (i, 0, 0), memory_space=pltpu.SMEM),
            pl.BlockSpec((tm, d), lambda i: (i, 0)),
            pl.BlockSpec((tm, TOP_K), lambda i: (i, 0)),
            pl.BlockSpec((1, d), lambda i: (0, 0)),
            pl.BlockSpec(memory_space=pl.ANY),
        ],
        out_specs=pl.BlockSpec((tm, d), lambda i: (i, 0)),
        out_shape=jax.ShapeDtypeStruct((tp, d), F32),
        scratch_shapes=[pltpu.VMEM((TOP_K, tm, d), F32), pltpu.SemaphoreType.DMA],
        compiler_params=_cparams("arbitrary"),
        name="moe_combine",
    )(slots, h, wcol, g, ys)


def _routing_tables(cnt, idx, rnk, n_items, tm):
    cnt = cnt.astype(I32)
    end = jnp.cumsum(cnt)
    offs = end - cnt
    experts = jnp.arange(N_EXPERTS, dtype=I32)

    def lookup(table, e):
        hit = e[None] == experts.reshape((N_EXPERTS,) + (1,) * e.ndim)
        return jnp.sum(jnp.where(hit, table.reshape(hit.shape[:1] + (1,) * e.ndim), 0), axis=0)

    slot = lookup(offs, idx) + rnk
    first_tile = offs // EXPERT_TILE
    n_e = jnp.where(cnt > 0, (end - 1) // EXPERT_TILE - first_tile + 1, 0)
    cum_items = jnp.cumsum(n_e)
    w = jnp.minimum(jnp.arange(n_items, dtype=I32), cum_items[-1] - 1)
    e_w = jnp.minimum(jnp.sum((w[None] >= cum_items[:, None]).astype(I32), axis=0), N_EXPERTS - 1)
    tile_w = lookup(first_tile, e_w) + (w - lookup(cum_items - n_e, e_w))
    base = tile_w * EXPERT_TILE
    lo_w = jnp.maximum(lookup(offs, e_w), base) - base
    hi_w = jnp.minimum(lookup(end, e_w), base + EXPERT_TILE) - base
    hi_w = jnp.where(jnp.arange(n_items) < cum_items[-1], hi_w, lo_w)
    tp = slot.shape[1]
    slots = slot.reshape(TOP_K, tp // tm, tm).transpose(1, 0, 2).reshape(tp // tm, 1, TOP_K * tm)
    return slots, tile_w.astype(I32), e_w, lo_w.astype(I32), hi_w.astype(I32)


def kernel(x, meta_tokens, norm_mix, w_in, ssm_lambda_re, ssm_lambda_im, ssm_log_dt, ssm_b_re, ssm_b_im, ssm_c_re, ssm_c_im, ssm_d, w_glu, b_glu, b_forget, norm_ssm_out, norm_attn_out, w_out, norm_ffn, w_router, b_router, w_gate, b_gate, w_up, b_up, w_down, b_down, norm_final):
    bsz, seq, d = x.shape
    depth = w_in.shape[0]
    l_real = N_META + seq
    lp = -(-l_real // SEQ_ALIGN) * SEQ_ALIGN
    tp = bsz * lp
    nc = lp // SSM_CHUNK
    tm_proj = 512 if tp % 512 == 0 else 256
    tm_moe = 256
    n_slots = TOP_K * tp
    n_items = n_slots // EXPERT_TILE + N_EXPERTS
    o_q, o_k, o_v, o_f = D_SSM, D_SSM + D_ATTN, D_SSM + 2 * D_ATTN, D_SSM + 3 * D_ATTN

    meta = jnp.broadcast_to(meta_tokens[None].astype(x.dtype), (bsz, N_META, d))
    pad = jnp.zeros((bsz, lp - l_real, d), x.dtype)
    h = jnp.concatenate([meta, x, pad], axis=1).reshape(tp, d)
    expert_bias = lambda b: b.reshape(depth, N_EXPERTS, 1, b.shape[-1])
    b_gate4, b_up4, b_down4 = expert_bias(b_gate), expert_bias(b_up), expert_bias(b_down)

    for l in range(depth):
        w_l = w_in[l]
        w_k = jnp.pad(w_l[:, o_k:o_v].reshape(d, N_HEADS, HEAD_DIM), ((0, 0), (0, 0), (0, AUG_ROWS)))
        w_a = jnp.concatenate([w_l[:, :o_q], w_k.reshape(d, N_HEADS * LANES),
                               jnp.pad(w_l[:, o_f:], ((0, 0), (0, LANES - N_HEADS)))], axis=1).astype(BF16)
        w_bt = jnp.concatenate([w_l[:, o_q:o_k], w_l[:, o_v:o_f]], axis=1).T.astype(BF16)
        bf = jnp.pad(b_forget[l].astype(F32), (0, LANES - N_HEADS)).reshape(1, LANES)
        u, ka, qa, vt = _in_proj(h.reshape(bsz, lp, d), norm_mix[l].reshape(1, d), w_a, w_bt, bf,
                                 SEQ_ALIGN, SEQ_ALIGN)

        wt, wb, wc, a16 = _ssm_operators(ssm_lambda_re[l], ssm_lambda_im[l], ssm_log_dt[l], ssm_b_re[l],
                                         ssm_b_im[l], ssm_c_re[l], ssm_c_im[l], ssm_d[l])
        ys = _ssm(u.reshape(N_BUNDLES, bsz, nc, BUNDLE_IN), wt, wb, wc, a16)
        ys = ys.reshape(N_BUNDLES, tp, LANES)

        ya = _attention(qa, ka, vt, norm_attn_out[l].reshape(1, D_ATTN), SEQ_ALIGN).reshape(tp, D_ATTN)

        wr = w_router[l].T.astype(F32)
        wrh = wr.astype(BF16)
        wrl = (wr - wrh.astype(F32)).astype(BF16)
        h, xn, idx, wts, rnk, cnt = _mix_route(
            h, ys, ya, w_glu[l].astype(BF16), b_glu[l].reshape(1, D_SSM), norm_ssm_out[l].reshape(1, D_SSM),
            w_out[l].astype(BF16), norm_ffn[l].reshape(1, d), wrh, wrl,
            b_router[l].astype(F32).reshape(N_EXPERTS, 1), tm_proj)

        slots, item_tile, item_expert, item_lo, item_hi = _routing_tables(cnt[:, 0], idx, rnk, n_items, tm_moe)
        xs = _dispatch(slots, xn, n_slots, tm_moe)
        yo = _experts(item_tile, item_expert, item_lo, item_hi, xs, w_gate, w_up, w_down,
                      b_gate4, b_up4, b_down4, l)
        h = _combine(slots, h, wts.T, norm_final.reshape(1, d), yo, tm_moe, l == depth - 1)

    return h.reshape(bsz, lp, d)[:, N_META:l_real]
```

```python
import functools
import math

import jax
import jax.numpy as jnp
from jax import lax
from jax.experimental import pallas as pl
from jax.experimental.pallas import tpu as pltpu

F32 = jnp.float32
BF16 = jnp.bfloat16
I32 = jnp.int32

N_META = 16
HEAD_DIM = 64
N_HEADS = 8
D_ATTN = N_HEADS * HEAD_DIM
D_SSM = 512
SSM_GROUP = 16
SSM_STATE = 64
N_GROUPS = D_SSM // SSM_GROUP
N_EXPERTS = 32
TOP_K = 4
SWIGLU_LIMIT = 7.0
SWIGLU_ALPHA = 1.702
RMS_EPS = 1e-6

LANES = 128
SEQ_ALIGN = 256
SSM_CHUNK = 16
GROUPS_PER_BUNDLE = LANES // SSM_GROUP
N_BUNDLES = N_GROUPS // GROUPS_PER_BUNDLE
BUNDLE_IN = SSM_CHUNK * LANES
BUNDLE_STATE = 2 * GROUPS_PER_BUNDLE * SSM_STATE
EXPERT_TILE = 256
VMEM_LIMIT = 56 * 1024 * 1024
NEG_BIG = -1e30

_NT = (((1,), (1,)), ((), ()))


def _cparams(*sem):
    return pltpu.CompilerParams(dimension_semantics=sem, vmem_limit_bytes=VMEM_LIMIT)


def _rms(x, g):
    return x * lax.rsqrt(jnp.mean(x * x, axis=-1, keepdims=True) + RMS_EPS) * g


def _sigmoid(x):
    return 1.0 / (1.0 + jnp.exp(-x))


N_SPLIT = 3
AUG_ROWS = LANES - HEAD_DIM


def _split3(x):
    x1 = x.astype(BF16)
    r1 = x - x1.astype(F32)
    x2 = r1.astype(BF16)
    x3 = (r1 - x2.astype(F32)).astype(BF16)
    return x1, x2, x3


def _in_proj_kernel(h_ref, g_ref, wa_ref, wbt_ref, bf_ref, tri_ref, place_ref,
                    u_ref, ka_ref, qa_ref, vt_ref, carry_ref, stage_ref):
    j = pl.program_id(1)
    tm = h_ref.shape[0]

    @pl.when(j == 0)
    def _():
        carry_ref[...] = jnp.zeros_like(carry_ref)

    xb = _rms(h_ref[...], g_ref[...]).astype(BF16)
    pa = jnp.dot(xb, wa_ref[...], preferred_element_type=F32)
    for gb in range(N_BUNDLES):
        stage_ref[...] = pa[:, gb * LANES:(gb + 1) * LANES]
        for t in range(SSM_CHUNK):
            u_ref[gb, :, t * LANES:(t + 1) * LANES] = stage_ref[
                pl.ds(t, tm // SSM_CHUNK, stride=SSM_CHUNK), :].astype(BF16)
    o_k = D_SSM
    o_f = D_SSM + N_HEADS * LANES

    z = pa[:, o_f:o_f + LANES] + bf_ref[...]
    lf = jnp.minimum(z, 0.0) - jnp.log(1.0 + jnp.exp(-jnp.abs(z)))
    tri = tri_ref[...]
    cs = carry_ref[...]
    for piece in _split3(lf):
        cs = cs + jnp.dot(tri, piece, preferred_element_type=F32)
    carry_ref[...] = cs[tm - 1:tm, :]
    aug = jnp.dot(jnp.concatenate(_split3(cs), axis=1), place_ref[...], preferred_element_type=F32)
    ka_ref[...] = (pa[:, o_k:o_f] + aug).astype(BF16)

    pbt = lax.dot_general(wbt_ref[...], xb, _NT, preferred_element_type=F32)
    row = lax.broadcasted_iota(I32, (AUG_ROWS, tm), 0)
    ones_rows = jnp.where(row < N_SPLIT, 1.0, 0.0).astype(BF16)
    for hd in range(N_HEADS):
        qa_ref[hd * LANES:hd * LANES + HEAD_DIM, :] = (
            pbt[hd * HEAD_DIM:(hd + 1) * HEAD_DIM, :] * (HEAD_DIM ** -0.5)).astype(BF16)
        qa_ref[hd * LANES + HEAD_DIM:(hd + 1) * LANES, :] = ones_rows
    tk = vt_ref.shape[2]
    for blk in range(tm // tk):
        vt_ref[blk] = pbt[D_ATTN:, blk * tk:(blk + 1) * tk].astype(BF16)


def _in_proj(h, g, wa, wbt, bf, tm, tk):
    bsz, lp, d = h.shape
    tri = (jnp.arange(tm)[:, None] >= jnp.arange(tm)[None, :]).astype(BF16)
    src = jnp.arange(N_SPLIT * LANES)
    dst = (src % LANES) * LANES + HEAD_DIM + src // LANES
    place = jnp.where((src % LANES < N_HEADS)[:, None] & (dst[:, None] == jnp.arange(N_HEADS * LANES)[None, :]),
                      -1.0, 0.0).astype(BF16)
    full = lambda a: pl.BlockSpec(a.shape, lambda b, j: (0,) * a.ndim)
    return pl.pallas_call(
        _in_proj_kernel,
        grid=(bsz, lp // tm),
        in_specs=[
            pl.BlockSpec((None, tm, d), lambda b, j: (b, j, 0)),
            full(g), full(wa), full(wbt), full(bf), full(tri), full(place),
        ],
        out_specs=[
            pl.BlockSpec((N_BUNDLES, None, tm // SSM_CHUNK, BUNDLE_IN), lambda b, j: (0, b, j, 0)),
            pl.BlockSpec((None, tm, N_HEADS * LANES), lambda b, j: (b, j, 0)),
            pl.BlockSpec((None, N_HEADS * LANES, tm), lambda b, j: (b, 0, j)),
            pl.BlockSpec((None, tm // tk, D_ATTN, tk), lambda b, j: (b, j, 0, 0)),
        ],
        out_shape=[
            jax.ShapeDtypeStruct((N_BUNDLES, bsz, lp // SSM_CHUNK, BUNDLE_IN), BF16),
            jax.ShapeDtypeStruct((bsz, lp, N_HEADS * LANES), BF16),
            jax.ShapeDtypeStruct((bsz, N_HEADS * LANES, lp), BF16),
            jax.ShapeDtypeStruct((bsz, lp // tk, D_ATTN, tk), BF16),
        ],
        scratch_shapes=[pltpu.VMEM((1, LANES), F32), pltpu.VMEM((tm, LANES), F32)],
        compiler_params=_cparams("parallel", "arbitrary"),
        name="in_proj",
    )(h, g, wa, wbt, bf, tri, place)


def _ssm_kernel(x_ref, wt_ref, wb_ref, wc_ref, a_ref, o_ref, z_ref, s_ref):
    x = x_ref[...]
    nc = x.shape[0]
    half = BUNDLE_STATE // 2
    z_ref[...] = jnp.dot(x, wb_ref[...], preferred_element_type=F32)
    a_re = a_ref[:, :half]
    a_im = a_ref[:, half:]

    def step(c8, carry):
        s_re, s_im = carry
        start = pl.multiple_of(c8 * 8, 8)
        zblk = z_ref[pl.ds(start, 8), :]
        rows = []
        for r in range(8):
            rows.append(jnp.concatenate([s_re, s_im], axis=1))
            z_re = zblk[r:r + 1, :half]
            z_im = zblk[r:r + 1, half:]
            s_re, s_im = (a_re * s_re - a_im * s_im + z_re,
                          a_re * s_im + a_im * s_re + z_im)
        s_ref[pl.ds(start, 8), :] = jnp.concatenate(rows, axis=0)
        return s_re, s_im

    zero = jnp.zeros((1, half), F32)
    lax.fori_loop(0, nc // 8, step, (zero, zero))
    y = jnp.dot(x, wt_ref[...], preferred_element_type=F32)
    y = y + jnp.dot(s_ref[...].astype(BF16), wc_ref[...], preferred_element_type=F32)
    for t in range(SSM_CHUNK):
        o_ref[pl.ds(t, nc, stride=SSM_CHUNK), :] = y[:, t * LANES:(t + 1) * LANES]


def _ssm(xb, wt, wb, wc, a16):
    nb, bsz, nc, _ = xb.shape
    return pl.pallas_call(
        _ssm_kernel,
        grid=(nb, bsz),
        in_specs=[
            pl.BlockSpec((None, None, nc, BUNDLE_IN), lambda g, b: (g, b, 0, 0)),
            pl.BlockSpec((None, BUNDLE_IN, BUNDLE_IN), lambda g, b: (g, 0, 0)),
            pl.BlockSpec((None, BUNDLE_IN, BUNDLE_STATE), lambda g, b: (g, 0, 0)),
            pl.BlockSpec((None, BUNDLE_STATE, BUNDLE_IN), lambda g, b: (g, 0, 0)),
            pl.BlockSpec((None, 1, BUNDLE_STATE), lambda g, b: (g, 0, 0)),
        ],
        out_specs=pl.BlockSpec((None, None, nc * SSM_CHUNK, LANES), lambda g, b: (g, b, 0, 0)),
        out_shape=jax.ShapeDtypeStruct((nb, bsz, nc * SSM_CHUNK, LANES), F32),
        scratch_shapes=[pltpu.VMEM((nc, BUNDLE_STATE), F32), pltpu.VMEM((nc, BUNDLE_STATE), F32)],
        compiler_params=_cparams("parallel", "parallel"),
        name="s5_chunked",
    )(xb, wt, wb, wc, a16)


def _ssm_operators(lam_re, lam_im, log_dt, b_re, b_im, c_re, c_im, d):
    g_, p_, h_, q_ = N_GROUPS, SSM_STATE, SSM_GROUP, SSM_CHUNK
    nb, g8 = N_BUNDLES, GROUPS_PER_BUNDLE
    hi = lax.Precision.HIGHEST
    lr, li = lam_re.astype(F32), lam_im.astype(F32)
    dt = jnp.exp(log_dt.astype(F32))[:, None]
    steps = jnp.arange(q_ + 1, dtype=F32)[:, None, None]
    mag = jnp.exp((lr * dt)[None] * steps)
    ang = (li * dt)[None] * steps
    pw_re, pw_im = mag * jnp.cos(ang), mag * jnp.sin(ang)
    n_re, n_im = pw_re[1] - 1.0, pw_im[1]
    den = lr * lr + li * li
    z_re, z_im = (n_re * lr + n_im * li) / den, (n_im * lr - n_re * li) / den
    br, bi = b_re.astype(F32), b_im.astype(F32)
    bb_re = z_re[:, :, None] * br - z_im[:, :, None] * bi
    bb_im = z_re[:, :, None] * bi + z_im[:, :, None] * br
    cr, ci = c_re.astype(F32), c_im.astype(F32)

    def blockdiag(x):
        r, c = x.shape[-2:]
        tile = jnp.tile(jnp.eye(c, dtype=F32), (1, g8))
        wide = jnp.einsum("...rc,cy->...ry", x.reshape(*x.shape[:-3], g8 * r, c), tile, precision=hi)
        same = (jnp.arange(g8 * r)[:, None] // r) == (jnp.arange(g8 * c)[None, :] // c)
        return jnp.where(same, wide, 0.0)

    cl_re = cr[None] * pw_re[:q_, :, None, :] - ci[None] * pw_im[:q_, :, None, :]
    cl_im = cr[None] * pw_im[:q_, :, None, :] + ci[None] * pw_re[:q_, :, None, :]
    kd = (jnp.einsum("kgep,gph->kghe", cl_re, bb_re, precision=hi)
          - jnp.einsum("kgep,gph->kghe", cl_im, bb_im, precision=hi))
    skip = jnp.eye(h_, dtype=F32)[None] * d.astype(F32).reshape(g_, h_, 1)
    kd = kd.at[0].add(skip)
    blocks = blockdiag(kd.reshape(q_, nb, g8, h_, h_)).astype(BF16)
    zero = jnp.zeros_like(blocks[:1])
    wt = jnp.stack([jnp.concatenate([zero] * t + [blocks[:q_ - t]], axis=0) for t in range(q_)])
    wt = wt.transpose(2, 0, 3, 1, 4).reshape(nb, BUNDLE_IN, BUNDLE_IN)

    rev_re, rev_im = pw_re[q_ - 1::-1][:q_], pw_im[q_ - 1::-1][:q_]
    bp_re = rev_re[:, :, :, None] * bb_re[None] - rev_im[:, :, :, None] * bb_im[None]
    bp_im = rev_re[:, :, :, None] * bb_im[None] + rev_im[:, :, :, None] * bb_re[None]
    to_hp = lambda x: x.transpose(0, 1, 3, 2).reshape(q_, nb, g8, h_, p_)
    wb = jnp.concatenate([blockdiag(to_hp(bp_re)), blockdiag(to_hp(bp_im))], axis=-1).astype(BF16)
    wb = wb.transpose(1, 0, 2, 3).reshape(nb, BUNDLE_IN, BUNDLE_STATE)

    cq_re = cr[None] * pw_re[1:, :, None, :] - ci[None] * pw_im[1:, :, None, :]
    cq_im = cr[None] * pw_im[1:, :, None, :] + ci[None] * pw_re[1:, :, None, :]
    to_ph = lambda x: x.transpose(0, 1, 3, 2).reshape(q_, nb, g8, p_, h_)
    wc = jnp.concatenate([blockdiag(to_ph(cq_re)), blockdiag(to_ph(-cq_im))], axis=-2).astype(BF16)
    wc = wc.transpose(1, 2, 0, 3).reshape(nb, BUNDLE_STATE, BUNDLE_IN)

    a16 = jnp.stack([pw_re[q_], pw_im[q_]]).reshape(2, nb, g8 * p_)
    a16 = a16.transpose(1, 0, 2).reshape(nb, 1, BUNDLE_STATE)
    return wt, wb, wc, a16


def _attn_kernel(qa_ref, ka_ref, vt_ref, g_ref, o_ref):
    tq = qa_ref.shape[1]
    tk = vt_ref.shape[2]
    i = pl.program_id(1)
    causal = lax.broadcasted_iota(I32, (tk, tq), 0) <= lax.broadcasted_iota(I32, (tk, tq), 1)

    def block(j, carry, masked):
        start = pl.multiple_of(j * tk, tk)
        scores = []
        for hd in range(N_HEADS):
            kb = ka_ref[pl.ds(start, tk), hd * LANES:(hd + 1) * LANES]
            scores.append(jnp.dot(kb, qa_ref[hd * LANES:(hd + 1) * LANES, :], preferred_element_type=F32))
        stats = []
        for hd in range(N_HEADS):
            m_old, l_old = carry[3 * hd:3 * hd + 2]
            s = scores[hd]
            if masked:
                s = jnp.where(causal, s, NEG_BIG)
            m_new = jnp.maximum(m_old, jnp.max(s, axis=0, keepdims=True))
            pr = jnp.exp(s - m_new)
            alpha = jnp.exp(m_old - m_new)
            l_new = alpha * l_old + jnp.sum(pr, axis=0, keepdims=True)
            stats.append((m_new, l_new, alpha, pr.astype(BF16)))
        new = []
        for hd in range(N_HEADS):
            m_new, l_new, alpha, pb = stats[hd]
            vb = vt_ref[j, hd * HEAD_DIM:(hd + 1) * HEAD_DIM, :]
            acc = alpha * carry[3 * hd + 2] + jnp.dot(vb, pb, preferred_element_type=F32)
            new += [m_new, l_new, acc]
        return tuple(new)

    init = (jnp.full((1, tq), NEG_BIG, F32), jnp.zeros((1, tq), F32), jnp.zeros((HEAD_DIM, tq), F32)) * N_HEADS
    carry = lax.fori_loop(0, i, functools.partial(block, masked=False), init)
    carry = block(i, carry, True)
    outs = [carry[3 * hd + 2] / carry[3 * hd + 1] for hd in range(N_HEADS)]
    y = jnp.concatenate(outs, axis=0).T
    o_ref[...] = _rms(y, g_ref[...]).astype(o_ref.dtype)


def _attention(qa, ka, vt, g, tq):
    bsz, lp, width = ka.shape
    nkv, _, tk = vt.shape[1:]
    assert tq == tk
    return pl.pallas_call(
        _attn_kernel,
        grid=(bsz, lp // tq),
        in_specs=[
            pl.BlockSpec((None, width, tq), lambda b, i: (b, 0, i)),
            pl.BlockSpec((None, lp, width), lambda b, i: (b, 0, 0)),
            pl.BlockSpec((None, nkv, D_ATTN, tk), lambda b, i: (b, 0, 0, 0)),
            pl.BlockSpec((1, D_ATTN), lambda b, i: (0, 0)),
        ],
        out_specs=pl.BlockSpec((None, tq, D_ATTN), lambda b, i: (b, i, 0)),
        out_shape=jax.ShapeDtypeStruct((bsz, lp, D_ATTN), BF16),
        compiler_params=_cparams("parallel", "parallel"),
        name="fox_attention",
    )(qa, ka, vt, g)


def _mix_kernel(h_ref, ys_ref, ya_ref, wglu_ref, bglu_ref, gs_ref, wout_ref, gf_ref,
                wrh_ref, wrl_ref, br_ref, tri_ref, real_ref,
                ho_ref, xn_ref, idx_ref, wts_ref, rnk_ref, cnt_ref, carry_ref):
    i = pl.program_id(0)
    tm = h_ref.shape[0]

    @pl.when(i == 0)
    def _():
        carry_ref[...] = jnp.zeros_like(carry_ref)

    ysm = jnp.concatenate([ys_ref[gb].astype(F32) for gb in range(N_BUNDLES)], axis=1)
    c0 = math.sqrt(2.0 / math.pi)
    gl = 0.5 * ysm * (1.0 + jnp.tanh(c0 * (ysm + 0.044715 * (ysm * ysm * ysm))))
    z = jnp.dot(gl.astype(BF16), wglu_ref[...], preferred_element_type=F32) + bglu_ref[...]
    y2 = gl * _sigmoid(z)
    mixed = jnp.concatenate([_rms(y2, gs_ref[...]).astype(BF16), ya_ref[...]], axis=1)
    hn = h_ref[...] + jnp.dot(mixed, wout_ref[...], preferred_element_type=F32)
    ho_ref[...] = hn
    xn = _rms(hn, gf_ref[...])
    xn_ref[...] = xn

    xh = xn.astype(BF16)
    xl = (xn - xh.astype(F32)).astype(BF16)
    wrh = wrh_ref[...]
    lg = (lax.dot_general(wrh, xh, _NT, preferred_element_type=F32)
          + lax.dot_general(wrh, xl, _NT, preferred_element_type=F32)
          + lax.dot_general(wrl_ref[...], xh, _NT, preferred_element_type=F32)) + br_ref[...]

    e_iota = lax.broadcasted_iota(I32, (N_EXPERTS, tm), 0)
    work = lg
    sel = jnp.zeros((N_EXPERTS, tm), jnp.bool_)
    top_i, top_v = [], []
    for _ in range(TOP_K):
        m = jnp.max(work, axis=0, keepdims=True)
        ik = jnp.min(jnp.where(work == m, e_iota, N_EXPERTS), axis=0, keepdims=True)
        hit = e_iota == ik
        sel = jnp.logical_or(sel, hit)
        work = jnp.where(hit, -jnp.inf, work)
        top_i.append(ik)
        top_v.append(m)
    tv = jnp.concatenate(top_v, axis=0)
    ew = jnp.exp(tv - tv[0:1])
    wts_ref[...] = ew / jnp.sum(ew, axis=0, keepdims=True)
    idx_ref[...] = jnp.concatenate(top_i, axis=0)

    sel = jnp.logical_and(sel, real_ref[...] > 0)
    onehot = sel.astype(BF16)
    rank = jnp.dot(onehot, tri_ref[...], preferred_element_type=F32) + carry_ref[:, 0:1]
    rnk_ref[...] = jnp.concatenate(
        [jnp.sum(jnp.where(e_iota == ik, rank, 0.0), axis=0, keepdims=True) for ik in top_i],
        axis=0).astype(I32)
    carry_ref[...] = carry_ref[...] + jnp.sum(sel.astype(F32), axis=1, keepdims=True)
    cnt_ref[...] = carry_ref[...]


def _mix_route(h, ys, ya, wglu, bglu, gs, wout, gf, wrh, wrl, br, real, tm):
    tp, d = h.shape
    tri = (jnp.arange(tm)[:, None] < jnp.arange(tm)[None, :]).astype(BF16)
    full = lambda shape: pl.BlockSpec(shape, lambda i: (0,) * len(shape))
    return pl.pallas_call(
        _mix_kernel,
        grid=(tp // tm,),
        in_specs=[
            pl.BlockSpec((tm, d), lambda i: (i, 0)),
            pl.BlockSpec((N_BUNDLES, tm, LANES), lambda i: (0, i, 0)),
            pl.BlockSpec((tm, D_ATTN), lambda i: (i, 0)),
            full((D_SSM, D_SSM)), full((1, D_SSM)), full((1, D_SSM)),
            full((d, d)), full((1, d)),
            full((N_EXPERTS, d)), full((N_EXPERTS, d)), full((N_EXPERTS, 1)),
            full((tm, tm)),
            pl.BlockSpec((1, tm), lambda i: (0, i)),
        ],
        out_specs=[
            pl.BlockSpec((tm, d), lambda i: (i, 0)),
            pl.BlockSpec((tm, d), lambda i: (i, 0)),
            pl.BlockSpec((TOP_K, tm), lambda i: (0, i)),
            pl.BlockSpec((TOP_K, tm), lambda i: (0, i)),
            pl.BlockSpec((TOP_K, tm), lambda i: (0, i)),
            pl.BlockSpec((N_EXPERTS, LANES), lambda i: (0, 0)),
        ],
        out_shape=[
            jax.ShapeDtypeStruct((tp, d), F32),
            jax.ShapeDtypeStruct((tp, d), F32),
            jax.ShapeDtypeStruct((TOP_K, tp), I32),
            jax.ShapeDtypeStruct((TOP_K, tp), F32),
            jax.ShapeDtypeStruct((TOP_K, tp), I32),
            jax.ShapeDtypeStruct((N_EXPERTS, LANES), F32),
        ],
        scratch_shapes=[pltpu.VMEM((N_EXPERTS, LANES), F32)],
        compiler_params=_cparams("arbitrary"),
        name="mix_route",
    )(h, ys, ya, wglu, bglu, gs, wout, gf, wrh, wrl, br, tri, real)


def _row_copy(src_ref, src_row, dst_ref, dst_row, sem):
    return pltpu.make_async_copy(src_ref.at[pl.ds(src_row, 1), :], dst_ref.at[pl.ds(dst_row, 1), :], sem)


def _real_rows(tm, tiles_per_seq, l_real):
    j = lax.rem(pl.program_id(0), tiles_per_seq)
    return jnp.clip(l_real - j * tm, 0, tm)


def _dispatch_kernel(slot_ref, xn_ref, xs_ref, sem, *, tiles_per_seq, l_real):
    tm = xn_ref.shape[0]
    n_rows = _real_rows(tm, tiles_per_seq, l_real)

    def issue(t, c):
        for k in range(TOP_K):
            _row_copy(xn_ref, t, xs_ref, slot_ref[0, k * tm + t], sem).start(priority=k % 2)
        return c

    lax.fori_loop(0, n_rows, issue, 0)

    def drain(t, c):
        for k in range(TOP_K):
            _row_copy(xn_ref, 0, xs_ref, 0, sem).wait()
        return c

    lax.fori_loop(0, n_rows, drain, 0)


def _dispatch(slots, xn, n_slots, tm, tiles_per_seq, l_real):
    tp, d = xn.shape
    return pl.pallas_call(
        functools.partial(_dispatch_kernel, tiles_per_seq=tiles_per_seq, l_real=l_real),
        grid=(tp // tm,),
        in_specs=[
            pl.BlockSpec((None, 1, TOP_K * tm), lambda i: (i, 0, 0), memory_space=pltpu.SMEM),
            pl.BlockSpec((tm, d), lambda i: (i, 0)),
        ],
        out_specs=pl.BlockSpec(memory_space=pl.ANY),
        out_shape=jax.ShapeDtypeStruct((n_slots, d), F32),
        scratch_shapes=[pltpu.SemaphoreType.DMA],
        compiler_params=_cparams("arbitrary"),
        name="moe_dispatch",
    )(slots, xn)


def _expert_kernel(wt_ref, we_ref, lo_ref, hi_ref, xs_ref, wg_ref, wu_ref, wd_ref, bg_ref, bu_ref, bd_ref,
                   ys_ref, wgb_ref, wub_ref, wdb_ref):
    w = pl.program_id(0)
    prev = jnp.maximum(w - 1, 0)
    lo, hi = lo_ref[w], hi_ref[w]
    valid = hi > lo
    new_expert = jnp.logical_or(w == 0, we_ref[w] != we_ref[prev])
    new_tile = jnp.logical_or(w == 0, wt_ref[w] != wt_ref[prev])

    @pl.when(jnp.logical_and(valid, new_expert))
    def _():
        wgb_ref[...] = wg_ref[...].astype(BF16)
        wub_ref[...] = wu_ref[...].astype(BF16)
        wdb_ref[...] = wd_ref[...].astype(BF16)

    def rows_of_item():
        x = xs_ref[...].astype(BF16)
        g = jnp.minimum(jnp.dot(x, wgb_ref[...], preferred_element_type=F32) + bg_ref[...], SWIGLU_LIMIT)
        u = jnp.clip(jnp.dot(x, wub_ref[...], preferred_element_type=F32) + bu_ref[...],
                     -SWIGLU_LIMIT, SWIGLU_LIMIT)
        hm = (g * _sigmoid(SWIGLU_ALPHA * g) * (u + 1.0)).astype(BF16)
        y = jnp.dot(hm, wdb_ref[...], preferred_element_type=F32) + bd_ref[...]
        row = lax.broadcasted_iota(I32, (EXPERT_TILE, 1), 0)
        return y, jnp.logical_and(row >= lo, row < hi)

    @pl.when(jnp.logical_and(valid, new_tile))
    def _():
        y, mine = rows_of_item()
        ys_ref[...] = jnp.where(mine, y, 0.0)

    @pl.when(jnp.logical_and(valid, jnp.logical_not(new_tile)))
    def _():
        y, mine = rows_of_item()
        ys_ref[...] = jnp.where(mine, y, ys_ref[...])


def _experts(item_tile, item_expert, item_lo, item_hi, xs, wg, wu, wd, bg, bu, bd, layer):
    n_slots, d = xs.shape
    dff = wg.shape[3]
    wspec = lambda r, c: pl.BlockSpec((None, None, r, c), lambda w, wt, we, lo, hi: (layer, we[w], 0, 0))
    tspec = pl.BlockSpec((EXPERT_TILE, d), lambda w, wt, we, lo, hi: (wt[w], 0))
    return pl.pallas_call(
        _expert_kernel,
        grid_spec=pltpu.PrefetchScalarGridSpec(
            num_scalar_prefetch=4,
            grid=(item_tile.shape[0],),
            in_specs=[
                tspec,
                wspec(d, dff), wspec(d, dff), wspec(dff, d),
                wspec(1, dff), wspec(1, dff), wspec(1, d),
            ],
            out_specs=tspec,
            scratch_shapes=[pltpu.VMEM((d, dff), BF16), pltpu.VMEM((d, dff), BF16), pltpu.VMEM((dff, d), BF16)],
        ),
        out_shape=jax.ShapeDtypeStruct((n_slots, d), F32),
        compiler_params=_cparams("arbitrary"),
        name="moe_experts",
    )(item_tile, item_expert, item_lo, item_hi, xs, wg, wu, wd, bg, bu, bd)


def _combine_kernel(slot_ref, h_ref, w_ref, g_ref, ys_ref, o_ref, buf_ref, sem, *,
                    final_norm, tiles_per_seq, l_real):
    tm = h_ref.shape[0]
    n_rows = _real_rows(tm, tiles_per_seq, l_real)

    @pl.when(pl.program_id(0) == 0)
    def _():
        buf_ref[...] = jnp.zeros_like(buf_ref)

    def issue(t, c):
        for k in range(TOP_K):
            _row_copy(ys_ref, slot_ref[0, k * tm + t], buf_ref.at[k], t, sem).start(priority=k % 2)
        return c

    lax.fori_loop(0, n_rows, issue, 0)

    def drain(t, c):
        for k in range(TOP_K):
            _row_copy(ys_ref, 0, buf_ref.at[k], 0, sem).wait()
        return c

    lax.fori_loop(0, n_rows, drain, 0)
    out = h_ref[...]
    for k in range(TOP_K):
        out = out + w_ref[:, k:k + 1] * buf_ref[k]
    if final_norm:
        out = _rms(out, g_ref[...])
    o_ref[...] = out


def _combine(slots, h, wcol, g, ys, tm, final_norm, tiles_per_seq, l_real):
    tp, d = h.shape
    return pl.pallas_call(
        functools.partial(_combine_kernel, final_norm=final_norm, tiles_per_seq=tiles_per_seq, l_real=l_real),
        grid=(tp // tm,),
        in_specs=[
            pl.BlockSpec((None, 1, TOP_K * tm), lambda i: (i, 0, 0), memory_space=pltpu.SMEM),
            pl.BlockSpec((tm, d), lambda i: (i, 0)),
            pl.BlockSpec((tm, TOP_K), lambda i: (i, 0)),
            pl.BlockSpec((1, d), lambda i: (0, 0)),
            pl.BlockSpec(memory_space=pl.ANY),
        ],
        out_specs=pl.BlockSpec((tm, d), lambda i: (i, 0)),
        out_shape=jax.ShapeDtypeStruct((tp, d), F32),
        scratch_shapes=[pltpu.VMEM((TOP_K, tm, d), F32), pltpu.SemaphoreType.DMA],
        compiler_params=_cparams("arbitrary"),
        name="moe_combine",
    )(slots, h, wcol, g, ys)


def _routing_tables(cnt, idx, rnk, n_items, tm):
    cnt = cnt.astype(I32)
    end = jnp.cumsum(cnt)
    offs = end - cnt
    experts = jnp.arange(N_EXPERTS, dtype=I32)

    def lookup(table, e):
        hit = e[None] == experts.reshape((N_EXPERTS,) + (1,) * e.ndim)
        return jnp.sum(jnp.where(hit, table.reshape(hit.shape[:1] + (1,) * e.ndim), 0), axis=0)

    slot = lookup(offs, idx) + rnk
    first_tile = offs // EXPERT_TILE
    n_e = jnp.where(cnt > 0, (end - 1) // EXPERT_TILE - first_tile + 1, 0)
    cum_items = jnp.cumsum(n_e)
    w = jnp.minimum(jnp.arange(n_items, dtype=I32), cum_items[-1] - 1)
    e_w = jnp.minimum(jnp.sum((w[None] >= cum_items[:, None]).astype(I32), axis=0), N_EXPERTS - 1)
    tile_w = lookup(first_tile, e_w) + (w - lookup(cum_items - n_e, e_w))
    base = tile_w * EXPERT_TILE
    lo_w = jnp.maximum(lookup(offs, e_w), base) - base
    hi_w = jnp.minimum(lookup(end, e_w), base + EXPERT_TILE) - base
    hi_w = jnp.where(jnp.arange(n_items) < cum_items[-1], hi_w, lo_w)
    tp = slot.shape[1]
    slots = slot.reshape(TOP_K, tp // tm, tm).transpose(1, 0, 2).reshape(tp // tm, 1, TOP_K * tm)
    return slots, tile_w.astype(I32), e_w, lo_w.astype(I32), hi_w.astype(I32)


def kernel(x, meta_tokens, norm_mix, w_in, ssm_lambda_re, ssm_lambda_im, ssm_log_dt, ssm_b_re, ssm_b_im, ssm_c_re, ssm_c_im, ssm_d, w_glu, b_glu, b_forget, norm_ssm_out, norm_attn_out, w_out, norm_ffn, w_router, b_router, w_gate, b_gate, w_up, b_up, w_down, b_down, norm_final):
    bsz, seq, d = x.shape
    depth = w_in.shape[0]
    l_real = N_META + seq
    lp = -(-l_real // SEQ_ALIGN) * SEQ_ALIGN
    tp = bsz * lp
    nc = lp // SSM_CHUNK
    tm_proj = 512 if tp % 512 == 0 else 256
    tm_moe = 256
    tiles_per_seq = lp // tm_moe
    n_slots = -(-(TOP_K * bsz * l_real) // EXPERT_TILE) * EXPERT_TILE
    n_items = n_slots // EXPERT_TILE + N_EXPERTS
    o_q, o_k, o_v, o_f = D_SSM, D_SSM + D_ATTN, D_SSM + 2 * D_ATTN, D_SSM + 3 * D_ATTN
    real = (jnp.arange(tp, dtype=I32) % lp < l_real).astype(I32).reshape(1, tp)

    meta = jnp.broadcast_to(meta_tokens[None].astype(x.dtype), (bsz, N_META, d))
    pad = jnp.zeros((bsz, lp - l_real, d), x.dtype)
    h = jnp.concatenate([meta, x, pad], axis=1).reshape(tp, d)
    expert_bias = lambda b: b.reshape(depth, N_EXPERTS, 1, b.shape[-1])
    b_gate4, b_up4, b_down4 = expert_bias(b_gate), expert_bias(b_up), expert_bias(b_down)

    for l in range(depth):
        w_l = w_in[l]
        w_k = jnp.pad(w_l[:, o_k:o_v].reshape(d, N_HEADS, HEAD_DIM), ((0, 0), (0, 0), (0, AUG_ROWS)))
        w_a = jnp.concatenate([w_l[:, :o_q], w_k.reshape(d, N_HEADS * LANES),
                               jnp.pad(w_l[:, o_f:], ((0, 0), (0, LANES - N_HEADS)))], axis=1).astype(BF16)
        w_bt = jnp.concatenate([w_l[:, o_q:o_k], w_l[:, o_v:o_f]], axis=1).T.astype(BF16)
        bf = jnp.pad(b_forget[l].astype(F32), (0, LANES - N_HEADS)).reshape(1, LANES)
        u, ka, qa, vt = _in_proj(h.reshape(bsz, lp, d), norm_mix[l].reshape(1, d), w_a, w_bt, bf,
                                 SEQ_ALIGN, SEQ_ALIGN)

        wt, wb, wc, a16 = _ssm_operators(ssm_lambda_re[l], ssm_lambda_im[l], ssm_log_dt[l], ssm_b_re[l],
                                         ssm_b_im[l], ssm_c_re[l], ssm_c_im[l], ssm_d[l])
        ys = _ssm(u, wt, wb, wc, a16).reshape(N_BUNDLES, tp, LANES)

        ya = _attention(qa, ka, vt, norm_attn_out[l].reshape(1, D_ATTN), SEQ_ALIGN).reshape(tp, D_ATTN)

        wr = w_router[l].T.astype(F32)
        wrh = wr.astype(BF16)
        wrl = (wr - wrh.astype(F32)).astype(BF16)
        h, xn, idx, wts, rnk, cnt = _mix_route(
            h, ys, ya, w_glu[l].astype(BF16), b_glu[l].reshape(1, D_SSM), norm_ssm_out[l].reshape(1, D_SSM),
            w_out[l].astype(BF16), norm_ffn[l].reshape(1, d), wrh, wrl,
            b_router[l].astype(F32).reshape(N_EXPERTS, 1), real, tm_proj)

        slots, item_tile, item_expert, item_lo, item_hi = _routing_tables(cnt[:, 0], idx, rnk, n_items, tm_moe)
        xs = _dispatch(slots, xn, n_slots, tm_moe, tiles_per_seq, l_real)
        yo = _experts(item_tile, item_expert, item_lo, item_hi, xs, w_gate, w_up, w_down,
                      b_gate4, b_up4, b_down4, l)
        wcol = jnp.where(real.reshape(tp, 1) > 0, wts.T, 0.0)
        h = _combine(slots, h, wcol, norm_final.reshape(1, d), yo, tm_moe, l == depth - 1,
                     tiles_per_seq, l_real)

    return h.reshape(bsz, lp, d)[:, N_META:l_real]
```

```python
import functools
import math

import jax
import jax.numpy as jnp
from jax import lax
from jax.experimental import pallas as pl
from jax.experimental.pallas import tpu as pltpu

F32 = jnp.float32
BF16 = jnp.bfloat16
I32 = jnp.int32

N_META = 16
HEAD_DIM = 64
N_HEADS = 8
D_ATTN = N_HEADS * HEAD_DIM
D_SSM = 512
SSM_GROUP = 16
SSM_STATE = 64
N_GROUPS = D_SSM // SSM_GROUP
N_EXPERTS = 32
TOP_K = 4
SWIGLU_LIMIT = 7.0
SWIGLU_ALPHA = 1.702
RMS_EPS = 1e-6

LANES = 128
SEQ_ALIGN = 256
SSM_CHUNK = 16
GROUPS_PER_BUNDLE = LANES // SSM_GROUP
N_BUNDLES = N_GROUPS // GROUPS_PER_BUNDLE
BUNDLE_IN = SSM_CHUNK * LANES
BUNDLE_STATE = 2 * GROUPS_PER_BUNDLE * SSM_STATE
EXPERT_TILE = 256
VMEM_LIMIT = 56 * 1024 * 1024
NEG_BIG = -1e30

_NT = (((1,), (1,)), ((), ()))


def _cparams(*sem):
    return pltpu.CompilerParams(dimension_semantics=sem, vmem_limit_bytes=VMEM_LIMIT)


def _rms(x, g):
    return x * lax.rsqrt(jnp.mean(x * x, axis=-1, keepdims=True) + RMS_EPS) * g


def _sigmoid(x):
    return 1.0 / (1.0 + jnp.exp(-x))


N_SPLIT = 3
AUG_ROWS = LANES - HEAD_DIM


def _split3(x):
    x1 = x.astype(BF16)
    r1 = x - x1.astype(F32)
    x2 = r1.astype(BF16)
    x3 = (r1 - x2.astype(F32)).astype(BF16)
    return x1, x2, x3


def _in_proj_kernel(h_ref, g_ref, wa_ref, wbt_ref, bf_ref, tri_ref, place_ref,
                    u_ref, ka_ref, qa_ref, vt_ref, carry_ref, stage_ref):
    j = pl.program_id(1)
    tm = h_ref.shape[0]

    @pl.when(j == 0)
    def _():
        carry_ref[...] = jnp.zeros_like(carry_ref)

    xb = _rms(h_ref[...], g_ref[...]).astype(BF16)
    pa = jnp.dot(xb, wa_ref[...], preferred_element_type=F32)
    for gb in range(N_BUNDLES):
        stage_ref[...] = pa[:, gb * LANES:(gb + 1) * LANES]
        for t in range(SSM_CHUNK):
            u_ref[gb, :, t * LANES:(t + 1) * LANES] = stage_ref[
                pl.ds(t, tm // SSM_CHUNK, stride=SSM_CHUNK), :].astype(BF16)
    o_k = D_SSM
    o_f = D_SSM + N_HEADS * LANES

    z = pa[:, o_f:o_f + LANES] + bf_ref[...]
    lf = jnp.minimum(z, 0.0) - jnp.log(1.0 + jnp.exp(-jnp.abs(z)))
    tri = tri_ref[...]
    cs = carry_ref[...]
    for piece in _split3(lf):
        cs = cs + jnp.dot(tri, piece, preferred_element_type=F32)
    carry_ref[...] = cs[tm - 1:tm, :]
    aug = jnp.dot(jnp.concatenate(_split3(cs), axis=1), place_ref[...], preferred_element_type=F32)
    ka_ref[...] = (pa[:, o_k:o_f] + aug).astype(BF16)

    pbt = lax.dot_general(wbt_ref[...], xb, _NT, preferred_element_type=F32)
    row = lax.broadcasted_iota(I32, (AUG_ROWS, tm), 0)
    ones_rows = jnp.where(row < N_SPLIT, 1.0, 0.0).astype(BF16)
    for hd in range(N_HEADS):
        qa_ref[hd * LANES:hd * LANES + HEAD_DIM, :] = (
            pbt[hd * HEAD_DIM:(hd + 1) * HEAD_DIM, :] * (HEAD_DIM ** -0.5)).astype(BF16)
        qa_ref[hd * LANES + HEAD_DIM:(hd + 1) * LANES, :] = ones_rows
    tk = vt_ref.shape[2]
    for blk in range(tm // tk):
        vt_ref[blk] = pbt[D_ATTN:, blk * tk:(blk + 1) * tk].astype(BF16)


def _in_proj(h, g, wa, wbt, bf, tm, tk):
    bsz, lp, d = h.shape
    tri = (jnp.arange(tm)[:, None] >= jnp.arange(tm)[None, :]).astype(BF16)
    src = jnp.arange(N_SPLIT * LANES)
    dst = (src % LANES) * LANES + HEAD_DIM + src // LANES
    place = jnp.where((src % LANES < N_HEADS)[:, None] & (dst[:, None] == jnp.arange(N_HEADS * LANES)[None, :]),
                      -1.0, 0.0).astype(BF16)
    full = lambda a: pl.BlockSpec(a.shape, lambda b, j: (0,) * a.ndim)
    return pl.pallas_call(
        _in_proj_kernel,
        grid=(bsz, lp // tm),
        in_specs=[
            pl.BlockSpec((None, tm, d), lambda b, j: (b, j, 0)),
            full(g), full(wa), full(wbt), full(bf), full(tri), full(place),
        ],
        out_specs=[
            pl.BlockSpec((N_BUNDLES, None, tm // SSM_CHUNK, BUNDLE_IN), lambda b, j: (0, b, j, 0)),
            pl.BlockSpec((None, tm, N_HEADS * LANES), lambda b, j: (b, j, 0)),
            pl.BlockSpec((None, N_HEADS * LANES, tm), lambda b, j: (b, 0, j)),
            pl.BlockSpec((None, tm // tk, D_ATTN, tk), lambda b, j: (b, j, 0, 0)),
        ],
        out_shape=[
            jax.ShapeDtypeStruct((N_BUNDLES, bsz, lp // SSM_CHUNK, BUNDLE_IN), BF16),
            jax.ShapeDtypeStruct((bsz, lp, N_HEADS * LANES), BF16),
            jax.ShapeDtypeStruct((bsz, N_HEADS * LANES, lp), BF16),
            jax.ShapeDtypeStruct((bsz, lp // tk, D_ATTN, tk), BF16),
        ],
        scratch_shapes=[pltpu.VMEM((1, LANES), F32), pltpu.VMEM((tm, LANES), F32)],
        compiler_params=_cparams("parallel", "arbitrary"),
        name="in_proj",
    )(h, g, wa, wbt, bf, tri, place)


def _ssm_kernel(x_ref, wt_ref, wb_ref, wc_ref, a_ref, o_ref, z_ref, s_ref):
    x = x_ref[...]
    nc = x.shape[0]
    half = BUNDLE_STATE // 2
    z_ref[...] = jnp.dot(x, wb_ref[...], preferred_element_type=F32)
    a_re = a_ref[:, :half]
    a_im = a_ref[:, half:]

    def step(c8, carry):
        s_re, s_im = carry
        start = pl.multiple_of(c8 * 8, 8)
        zblk = z_ref[pl.ds(start, 8), :]
        rows = []
        for r in range(8):
            rows.append(jnp.concatenate([s_re, s_im], axis=1))
            z_re = zblk[r:r + 1, :half]
            z_im = zblk[r:r + 1, half:]
            s_re, s_im = (a_re * s_re - a_im * s_im + z_re,
                          a_re * s_im + a_im * s_re + z_im)
        s_ref[pl.ds(start, 8), :] = jnp.concatenate(rows, axis=0)
        return s_re, s_im

    zero = jnp.zeros((1, half), F32)
    lax.fori_loop(0, nc // 8, step, (zero, zero))
    y = jnp.dot(x, wt_ref[...], preferred_element_type=F32)
    y = y + jnp.dot(s_ref[...].astype(BF16), wc_ref[...], preferred_element_type=F32)
    for t in range(SSM_CHUNK):
        o_ref[pl.ds(t, nc, stride=SSM_CHUNK), :] = y[:, t * LANES:(t + 1) * LANES]


def _ssm(xb, wt, wb, wc, a16):
    nb, bsz, nc, _ = xb.shape
    return pl.pallas_call(
        _ssm_kernel,
        grid=(nb, bsz),
        in_specs=[
            pl.BlockSpec((None, None, nc, BUNDLE_IN), lambda g, b: (g, b, 0, 0)),
            pl.BlockSpec((None, BUNDLE_IN, BUNDLE_IN), lambda g, b: (g, 0, 0)),
            pl.BlockSpec((None, BUNDLE_IN, BUNDLE_STATE), lambda g, b: (g, 0, 0)),
            pl.BlockSpec((None, BUNDLE_STATE, BUNDLE_IN), lambda g, b: (g, 0, 0)),
            pl.BlockSpec((None, 1, BUNDLE_STATE), lambda g, b: (g, 0, 0)),
        ],
        out_specs=pl.BlockSpec((None, None, nc * SSM_CHUNK, LANES), lambda g, b: (g, b, 0, 0)),
        out_shape=jax.ShapeDtypeStruct((nb, bsz, nc * SSM_CHUNK, LANES), F32),
        scratch_shapes=[pltpu.VMEM((nc, BUNDLE_STATE), F32), pltpu.VMEM((nc, BUNDLE_STATE), F32)],
        compiler_params=_cparams("parallel", "parallel"),
        name="s5_chunked",
    )(xb, wt, wb, wc, a16)


def _ssm_operators(lam_re, lam_im, log_dt, b_re, b_im, c_re, c_im, d):
    g_, p_, h_, q_ = N_GROUPS, SSM_STATE, SSM_GROUP, SSM_CHUNK
    nb, g8 = N_BUNDLES, GROUPS_PER_BUNDLE
    hi = lax.Precision.HIGHEST
    lr, li = lam_re.astype(F32), lam_im.astype(F32)
    dt = jnp.exp(log_dt.astype(F32))[:, None]
    steps = jnp.arange(q_ + 1, dtype=F32)[:, None, None]
    mag = jnp.exp((lr * dt)[None] * steps)
    ang = (li * dt)[None] * steps
    pw_re, pw_im = mag * jnp.cos(ang), mag * jnp.sin(ang)
    n_re, n_im = pw_re[1] - 1.0, pw_im[1]
    den = lr * lr + li * li
    z_re, z_im = (n_re * lr + n_im * li) / den, (n_im * lr - n_re * li) / den
    br, bi = b_re.astype(F32), b_im.astype(F32)
    bb_re = z_re[:, :, None] * br - z_im[:, :, None] * bi
    bb_im = z_re[:, :, None] * bi + z_im[:, :, None] * br
    cr, ci = c_re.astype(F32), c_im.astype(F32)

    def blockdiag(x):
        r, c = x.shape[-2:]
        tile = jnp.tile(jnp.eye(c, dtype=F32), (1, g8))
        wide = jnp.einsum("...rc,cy->...ry", x.reshape(*x.shape[:-3], g8 * r, c), tile, precision=hi)
        same = (jnp.arange(g8 * r)[:, None] // r) == (jnp.arange(g8 * c)[None, :] // c)
        return jnp.where(same, wide, 0.0)

    cl_re = cr[None] * pw_re[:q_, :, None, :] - ci[None] * pw_im[:q_, :, None, :]
    cl_im = cr[None] * pw_im[:q_, :, None, :] + ci[None] * pw_re[:q_, :, None, :]
    kd = (jnp.einsum("kgep,gph->kghe", cl_re, bb_re, precision=hi)
          - jnp.einsum("kgep,gph->kghe", cl_im, bb_im, precision=hi))
    skip = jnp.eye(h_, dtype=F32)[None] * d.astype(F32).reshape(g_, h_, 1)
    kd = kd.at[0].add(skip)
    blocks = blockdiag(kd.reshape(q_, nb, g8, h_, h_)).astype(BF16)
    zero = jnp.zeros_like(blocks[:1])
    wt = jnp.stack([jnp.concatenate([zero] * t + [blocks[:q_ - t]], axis=0) for t in range(q_)])
    wt = wt.transpose(2, 0, 3, 1, 4).reshape(nb, BUNDLE_IN, BUNDLE_IN)

    rev_re, rev_im = pw_re[q_ - 1::-1][:q_], pw_im[q_ - 1::-1][:q_]
    bp_re = rev_re[:, :, :, None] * bb_re[None] - rev_im[:, :, :, None] * bb_im[None]
    bp_im = rev_re[:, :, :, None] * bb_im[None] + rev_im[:, :, :, None] * bb_re[None]
    to_hp = lambda x: x.transpose(0, 1, 3, 2).reshape(q_, nb, g8, h_, p_)
    wb = jnp.concatenate([blockdiag(to_hp(bp_re)), blockdiag(to_hp(bp_im))], axis=-1).astype(BF16)
    wb = wb.transpose(1, 0, 2, 3).reshape(nb, BUNDLE_IN, BUNDLE_STATE)

    cq_re = cr[None] * pw_re[1:, :, None, :] - ci[None] * pw_im[1:, :, None, :]
    cq_im = cr[None] * pw_im[1:, :, None, :] + ci[None] * pw_re[1:, :, None, :]
    to_ph = lambda x: x.transpose(0, 1, 3, 2).reshape(q_, nb, g8, p_, h_)
    wc = jnp.concatenate([blockdiag(to_ph(cq_re)), blockdiag(to_ph(-cq_im))], axis=-2).astype(BF16)
    wc = wc.transpose(1, 2, 0, 3).reshape(nb, BUNDLE_STATE, BUNDLE_IN)

    a16 = jnp.stack([pw_re[q_], pw_im[q_]]).reshape(2, nb, g8 * p_)
    a16 = a16.transpose(1, 0, 2).reshape(nb, 1, BUNDLE_STATE)
    return wt, wb, wc, a16


def _attn_kernel(qa_ref, ka_ref, vt_ref, g_ref, o_ref):
    tq = qa_ref.shape[1]
    tk = vt_ref.shape[2]
    i = pl.program_id(1)
    causal = lax.broadcasted_iota(I32, (tk, tq), 0) <= lax.broadcasted_iota(I32, (tk, tq), 1)

    def block(j, carry, masked):
        start = pl.multiple_of(j * tk, tk)
        scores = []
        for hd in range(N_HEADS):
            kb = ka_ref[pl.ds(start, tk), hd * LANES:(hd + 1) * LANES]
            scores.append(jnp.dot(kb, qa_ref[hd * LANES:(hd + 1) * LANES, :], preferred_element_type=F32))
        stats = []
        for hd in range(N_HEADS):
            m_old, l_old = carry[3 * hd:3 * hd + 2]
            s = scores[hd]
            if masked:
                s = jnp.where(causal, s, NEG_BIG)
            m_new = jnp.maximum(m_old, jnp.max(s, axis=0, keepdims=True))
            pr = jnp.exp(s - m_new)
            alpha = jnp.exp(m_old - m_new)
            l_new = alpha * l_old + jnp.sum(pr, axis=0, keepdims=True)
            stats.append((m_new, l_new, alpha, pr.astype(BF16)))
        new = []
        for hd in range(N_HEADS):
            m_new, l_new, alpha, pb = stats[hd]
            vb = vt_ref[j, hd * HEAD_DIM:(hd + 1) * HEAD_DIM, :]
            acc = alpha * carry[3 * hd + 2] + jnp.dot(vb, pb, preferred_element_type=F32)
            new += [m_new, l_new, acc]
        return tuple(new)

    init = (jnp.full((1, tq), NEG_BIG, F32), jnp.zeros((1, tq), F32), jnp.zeros((HEAD_DIM, tq), F32)) * N_HEADS
    carry = lax.fori_loop(0, i, functools.partial(block, masked=False), init)
    carry = block(i, carry, True)
    outs = [carry[3 * hd + 2] / carry[3 * hd + 1] for hd in range(N_HEADS)]
    y = jnp.concatenate(outs, axis=0).T
    o_ref[...] = _rms(y, g_ref[...]).astype(o_ref.dtype)


def _attention(qa, ka, vt, g, tq):
    bsz, lp, width = ka.shape
    nkv, _, tk = vt.shape[1:]
    assert tq == tk
    return pl.pallas_call(
        _attn_kernel,
        grid=(bsz, lp // tq),
        in_specs=[
            pl.BlockSpec((None, width, tq), lambda b, i: (b, 0, i)),
            pl.BlockSpec((None, lp, width), lambda b, i: (b, 0, 0)),
            pl.BlockSpec((None, nkv, D_ATTN, tk), lambda b, i: (b, 0, 0, 0)),
            pl.BlockSpec((1, D_ATTN), lambda b, i: (0, 0)),
        ],
        out_specs=pl.BlockSpec((None, tq, D_ATTN), lambda b, i: (b, i, 0)),
        out_shape=jax.ShapeDtypeStruct((bsz, lp, D_ATTN), BF16),
        compiler_params=_cparams("parallel", "parallel"),
        name="fox_attention",
    )(qa, ka, vt, g)


def _mix_kernel(h_ref, ys_ref, ya_ref, wglu_ref, bglu_ref, gs_ref, wout_ref, gf_ref,
                wrh_ref, wrl_ref, br_ref, tri_ref,
                ho_ref, xn_ref, idx_ref, wts_ref, rnk_ref, cnt_ref, carry_ref):
    i = pl.program_id(0)
    tm = h_ref.shape[0]

    @pl.when(i == 0)
    def _():
        carry_ref[...] = jnp.zeros_like(carry_ref)

    ysm = jnp.concatenate([ys_ref[gb].astype(F32) for gb in range(N_BUNDLES)], axis=1)
    c0 = math.sqrt(2.0 / math.pi)
    gl = 0.5 * ysm * (1.0 + jnp.tanh(c0 * (ysm + 0.044715 * (ysm * ysm * ysm))))
    z = jnp.dot(gl.astype(BF16), wglu_ref[...], preferred_element_type=F32) + bglu_ref[...]
    y2 = gl * _sigmoid(z)
    mixed = jnp.concatenate([_rms(y2, gs_ref[...]).astype(BF16), ya_ref[...]], axis=1)
    hn = h_ref[...] + jnp.dot(mixed, wout_ref[...], preferred_element_type=F32)
    ho_ref[...] = hn
    xn = _rms(hn, gf_ref[...])
    xn_ref[...] = xn

    xh = xn.astype(BF16)
    xl = (xn - xh.astype(F32)).astype(BF16)
    wrh = wrh_ref[...]
    lg = (lax.dot_general(wrh, xh, _NT, preferred_element_type=F32)
          + lax.dot_general(wrh, xl, _NT, preferred_element_type=F32)
          + lax.dot_general(wrl_ref[...], xh, _NT, preferred_element_type=F32)) + br_ref[...]

    e_iota = lax.broadcasted_iota(I32, (N_EXPERTS, tm), 0)
    work = lg
    sel = jnp.zeros((N_EXPERTS, tm), jnp.bool_)
    top_i, top_v = [], []
    for _ in range(TOP_K):
        m = jnp.max(work, axis=0, keepdims=True)
        ik = jnp.min(jnp.where(work == m, e_iota, N_EXPERTS), axis=0, keepdims=True)
        hit = e_iota == ik
        sel = jnp.logical_or(sel, hit)
        work = jnp.where(hit, -jnp.inf, work)
        top_i.append(ik)
        top_v.append(m)
    tv = jnp.concatenate(top_v, axis=0)
    ew = jnp.exp(tv - tv[0:1])
    wts_ref[...] = ew / jnp.sum(ew, axis=0, keepdims=True)
    idx_ref[...] = jnp.concatenate(top_i, axis=0)

    onehot = sel.astype(BF16)
    rank = jnp.dot(onehot, tri_ref[...], preferred_element_type=F32) + carry_ref[:, 0:1]
    rnk_ref[...] = jnp.concatenate(
        [jnp.sum(jnp.where(e_iota == ik, rank, 0.0), axis=0, keepdims=True) for ik in top_i],
        axis=0).astype(I32)
    carry_ref[...] = carry_ref[...] + jnp.sum(sel.astype(F32), axis=1, keepdims=True)
    cnt_ref[...] = carry_ref[...]


def _mix_route(h, ys, ya, wglu, bglu, gs, wout, gf, wrh, wrl, br, tm):
    tp, d = h.shape
    tri = (jnp.arange(tm)[:, None] < jnp.arange(tm)[None, :]).astype(BF16)
    full = lambda shape: pl.BlockSpec(shape, lambda i: (0,) * len(shape))
    return pl.pallas_call(
        _mix_kernel,
        grid=(tp // tm,),
        in_specs=[
            pl.BlockSpec((tm, d), lambda i: (i, 0)),
            pl.BlockSpec((N_BUNDLES, tm, LANES), lambda i: (0, i, 0)),
            pl.BlockSpec((tm, D_ATTN), lambda i: (i, 0)),
            full((D_SSM, D_SSM)), full((1, D_SSM)), full((1, D_SSM)),
            full((d, d)), full((1, d)),
            full((N_EXPERTS, d)), full((N_EXPERTS, d)), full((N_EXPERTS, 1)),
            full((tm, tm)),
        ],
        out_specs=[
            pl.BlockSpec((tm, d), lambda i: (i, 0)),
            pl.BlockSpec((tm, d), lambda i: (i, 0)),
            pl.BlockSpec((TOP_K, tm), lambda i: (0, i)),
            pl.BlockSpec((TOP_K, tm), lambda i: (0, i)),
            pl.BlockSpec((TOP_K, tm), lambda i: (0, i)),
            pl.BlockSpec((N_EXPERTS, LANES), lambda i: (0, 0)),
        ],
        out_shape=[
            jax.ShapeDtypeStruct((tp, d), F32),
            jax.ShapeDtypeStruct((tp, d), F32),
            jax.ShapeDtypeStruct((TOP_K, tp), I32),
            jax.ShapeDtypeStruct((TOP_K, tp), F32),
            jax.ShapeDtypeStruct((TOP_K, tp), I32),
            jax.ShapeDtypeStruct((N_EXPERTS, LANES), F32),
        ],
        scratch_shapes=[pltpu.VMEM((N_EXPERTS, LANES), F32)],
        compiler_params=_cparams("arbitrary"),
        name="mix_route",
    )(h, ys, ya, wglu, bglu, gs, wout, gf, wrh, wrl, br, tri)


def _row_dma(src_ref, src_row, dst_ref, dst_row, sem):
    return pltpu.make_async_copy(src_ref.at[pl.ds(src_row, 1), :], dst_ref.at[pl.ds(dst_row, 1), :], sem)


def _expert_kernel(wt_ref, we_ref, lo_ref, hi_ref, tok_ref, tok_next_ref, pair_ref, pair_prev_ref,
                   wg_ref, wu_ref, wd_ref, bg_ref, bu_ref, bd_ref, xn_ref,
                   yp_ref, wgb_ref, wub_ref, wdb_ref, xbuf_ref, ybuf_ref, gsem, ssem):
    w = pl.program_id(0)
    last_w = pl.num_programs(0) - 1
    tile = wt_ref[w]
    slot = lax.rem(tile, 2)
    other = 1 - slot
    prev = jnp.maximum(w - 1, 0)
    lo, hi = lo_ref[w], hi_ref[w]
    new_expert = jnp.logical_or(w == 0, we_ref[w] != we_ref[prev])
    first = jnp.logical_or(w == 0, wt_ref[prev] != tile)
    rows = range(EXPERT_TILE)

    def gather(tokens_ref, s):
        for r in rows:
            _row_dma(xn_ref, tokens_ref[0, r], xbuf_ref.at[s], r, gsem.at[s]).start(priority=r % 2)

    def scatter(s, pairs_ref):
        for r in rows:
            _row_dma(ybuf_ref.at[s], r, yp_ref, pairs_ref[0, r], ssem.at[s]).start(priority=r % 2)

    def wait_gather(s):
        for r in rows:
            _row_dma(xn_ref, 0, xbuf_ref.at[s], 0, gsem.at[s]).wait()

    def wait_scatter(s):
        for r in rows:
            _row_dma(ybuf_ref.at[s], 0, yp_ref, 0, ssem.at[s]).wait()

    @pl.when(w == 0)
    def _():
        gather(tok_ref, slot)
        ybuf_ref[...] = jnp.zeros_like(ybuf_ref)
        scatter(slot, pair_ref)

    @pl.when(new_expert)
    def _():
        wgb_ref[...] = wg_ref[...].astype(BF16)
        wub_ref[...] = wu_ref[...].astype(BF16)
        wdb_ref[...] = wd_ref[...].astype(BF16)

    def item(is_first):
        if is_first:
            wait_gather(slot)
            wait_scatter(slot)
            scatter(other, pair_prev_ref)
            gather(tok_next_ref, other)
        x = xbuf_ref[slot].astype(BF16)
        g = jnp.minimum(jnp.dot(x, wgb_ref[...], preferred_element_type=F32) + bg_ref[...], SWIGLU_LIMIT)
        u = jnp.clip(jnp.dot(x, wub_ref[...], preferred_element_type=F32) + bu_ref[...],
                     -SWIGLU_LIMIT, SWIGLU_LIMIT)
        hm = (g * _sigmoid(SWIGLU_ALPHA * g) * (u + 1.0)).astype(BF16)
        y = jnp.dot(hm, wdb_ref[...], preferred_element_type=F32) + bd_ref[...]
        row = lax.broadcasted_iota(I32, (EXPERT_TILE, 1), 0)
        mine = jnp.logical_and(row >= lo, row < hi)
        ybuf_ref[slot] = jnp.where(mine, y, 0.0 if is_first else ybuf_ref[slot])

    pl.when(first)(functools.partial(item, True))
    pl.when(jnp.logical_not(first))(functools.partial(item, False))

    @pl.when(w == last_w)
    def _():
        scatter(slot, pair_ref)
        wait_scatter(slot)
        wait_scatter(other)
        wait_gather(other)


def _experts(item_tile, item_expert, item_lo, item_hi, tokens, pairs, xn, wg, wu, wd, bg, bu, bd, layer):
    n_tiles = tokens.shape[0]
    d = xn.shape[1]
    dff = wg.shape[3]
    wspec = lambda r, c: pl.BlockSpec((None, None, r, c), lambda w, wt, we, lo, hi: (layer, we[w], 0, 0))
    ids = lambda shift: pl.BlockSpec(
        (None, 1, EXPERT_TILE), lambda w, wt, we, lo, hi: (jnp.clip(wt[w] + shift, 0, n_tiles - 1), 0, 0),
        memory_space=pltpu.SMEM)
    return pl.pallas_call(
        _expert_kernel,
        grid_spec=pltpu.PrefetchScalarGridSpec(
            num_scalar_prefetch=4,
            grid=(item_tile.shape[0],),
            in_specs=[
                ids(0), ids(1), ids(0), ids(-1),
                wspec(d, dff), wspec(d, dff), wspec(dff, d),
                wspec(1, dff), wspec(1, dff), wspec(1, d),
                pl.BlockSpec(memory_space=pl.ANY),
            ],
            out_specs=pl.BlockSpec(memory_space=pl.ANY),
            scratch_shapes=[pltpu.VMEM((d, dff), BF16), pltpu.VMEM((d, dff), BF16), pltpu.VMEM((dff, d), BF16),
                            pltpu.VMEM((2, EXPERT_TILE, d), F32), pltpu.VMEM((2, EXPERT_TILE, d), F32),
                            pltpu.SemaphoreType.DMA((2,)), pltpu.SemaphoreType.DMA((2,))],
        ),
        out_shape=jax.ShapeDtypeStruct((n_tiles * EXPERT_TILE, d), F32),
        compiler_params=_cparams("arbitrary"),
        name="moe_experts",
    )(item_tile, item_expert, item_lo, item_hi, tokens, tokens, pairs, pairs, wg, wu, wd, bg, bu, bd, xn)


def _combine_kernel(h_ref, w_ref, g_ref, yp_ref, o_ref, *, final_norm):
    out = h_ref[...]
    for k in range(TOP_K):
        out = out + w_ref[:, k:k + 1] * yp_ref[k]
    if final_norm:
        out = _rms(out, g_ref[...])
    o_ref[...] = out


def _combine(h, wcol, g, yp, tm, final_norm):
    tp, d = h.shape
    return pl.pallas_call(
        functools.partial(_combine_kernel, final_norm=final_norm),
        grid=(tp // tm,),
        in_specs=[
            pl.BlockSpec((tm, d), lambda i: (i, 0)),
            pl.BlockSpec((tm, TOP_K), lambda i: (i, 0)),
            pl.BlockSpec((1, d), lambda i: (0, 0)),
            pl.BlockSpec((TOP_K, tm, d), lambda i: (0, i, 0)),
        ],
        out_specs=pl.BlockSpec((tm, d), lambda i: (i, 0)),
        out_shape=jax.ShapeDtypeStruct((tp, d), F32),
        compiler_params=_cparams("parallel"),
        name="moe_combine",
    )(h, wcol, g, yp)


def _routing_tables(cnt, idx, n_items):
    cnt = cnt.astype(I32)
    end = jnp.cumsum(cnt)
    offs = end - cnt
    experts = jnp.arange(N_EXPERTS, dtype=I32)

    def lookup(table, e):
        hit = e[None] == experts.reshape((N_EXPERTS,) + (1,) * e.ndim)
        return jnp.sum(jnp.where(hit, table.reshape(hit.shape[:1] + (1,) * e.ndim), 0), axis=0)

    tp = idx.shape[1]
    n_pairs = TOP_K * tp
    pair = jnp.sort(idx.reshape(n_pairs) * n_pairs + jnp.arange(n_pairs, dtype=I32)) % n_pairs
    pairs = pair.reshape(n_pairs // EXPERT_TILE, 1, EXPERT_TILE)
    tokens = pairs % tp
    first_tile = offs // EXPERT_TILE
    n_e = jnp.where(cnt > 0, (end - 1) // EXPERT_TILE - first_tile + 1, 0)
    cum_items = jnp.cumsum(n_e)
    w = jnp.minimum(jnp.arange(n_items, dtype=I32), cum_items[-1] - 1)
    e_w = jnp.minimum(jnp.sum((w[None] >= cum_items[:, None]).astype(I32), axis=0), N_EXPERTS - 1)
    tile_w = lookup(first_tile, e_w) + (w - lookup(cum_items - n_e, e_w))
    base = tile_w * EXPERT_TILE
    lo_w = jnp.maximum(lookup(offs, e_w), base) - base
    hi_w = jnp.minimum(lookup(end, e_w), base + EXPERT_TILE) - base
    hi_w = jnp.where(jnp.arange(n_items) < cum_items[-1], hi_w, lo_w)
    return tokens, pairs, tile_w.astype(I32), e_w, lo_w.astype(I32), hi_w.astype(I32)


def kernel(x, meta_tokens, norm_mix, w_in, ssm_lambda_re, ssm_lambda_im, ssm_log_dt, ssm_b_re, ssm_b_im, ssm_c_re, ssm_c_im, ssm_d, w_glu, b_glu, b_forget, norm_ssm_out, norm_attn_out, w_out, norm_ffn, w_router, b_router, w_gate, b_gate, w_up, b_up, w_down, b_down, norm_final):
    bsz, seq, d = x.shape
    depth = w_in.shape[0]
    l_real = N_META + seq
    lp = -(-l_real // SEQ_ALIGN) * SEQ_ALIGN
    tp = bsz * lp
    nc = lp // SSM_CHUNK
    tm_proj = 512 if tp % 512 == 0 else 256
    tm_moe = 256
    n_slots = TOP_K * tp
    n_items = n_slots // EXPERT_TILE + N_EXPERTS
    o_q, o_k, o_v, o_f = D_SSM, D_SSM + D_ATTN, D_SSM + 2 * D_ATTN, D_SSM + 3 * D_ATTN

    meta = jnp.broadcast_to(meta_tokens[None].astype(x.dtype), (bsz, N_META, d))
    pad = jnp.zeros((bsz, lp - l_real, d), x.dtype)
    h = jnp.concatenate([meta, x, pad], axis=1).reshape(tp, d)
    expert_bias = lambda b: b.reshape(depth, N_EXPERTS, 1, b.shape[-1])
    b_gate4, b_up4, b_down4 = expert_bias(b_gate), expert_bias(b_up), expert_bias(b_down)

    for l in range(depth):
        w_l = w_in[l]
        w_k = jnp.pad(w_l[:, o_k:o_v].reshape(d, N_HEADS, HEAD_DIM), ((0, 0), (0, 0), (0, AUG_ROWS)))
        w_a = jnp.concatenate([w_l[:, :o_q], w_k.reshape(d, N_HEADS * LANES),
                               jnp.pad(w_l[:, o_f:], ((0, 0), (0, LANES - N_HEADS)))], axis=1).astype(BF16)
        w_bt = jnp.concatenate([w_l[:, o_q:o_k], w_l[:, o_v:o_f]], axis=1).T.astype(BF16)
        bf = jnp.pad(b_forget[l].astype(F32), (0, LANES - N_HEADS)).reshape(1, LANES)
        u, ka, qa, vt = _in_proj(h.reshape(bsz, lp, d), norm_mix[l].reshape(1, d), w_a, w_bt, bf,
                                 SEQ_ALIGN, SEQ_ALIGN)

        wt, wb, wc, a16 = _ssm_operators(ssm_lambda_re[l], ssm_lambda_im[l], ssm_log_dt[l], ssm_b_re[l],
                                         ssm_b_im[l], ssm_c_re[l], ssm_c_im[l], ssm_d[l])
        ys = _ssm(u, wt, wb, wc, a16).reshape(N_BUNDLES, tp, LANES)

        ya = _attention(qa, ka, vt, norm_attn_out[l].reshape(1, D_ATTN), SEQ_ALIGN).reshape(tp, D_ATTN)

        wr = w_router[l].T.astype(F32)
        wrh = wr.astype(BF16)
        wrl = (wr - wrh.astype(F32)).astype(BF16)
        h, xn, idx, wts, rnk, cnt = _mix_route(
            h, ys, ya, w_glu[l].astype(BF16), b_glu[l].reshape(1, D_SSM), norm_ssm_out[l].reshape(1, D_SSM),
            w_out[l].astype(BF16), norm_ffn[l].reshape(1, d), wrh, wrl,
            b_router[l].astype(F32).reshape(N_EXPERTS, 1), tm_proj)

        tokens, pairs, item_tile, item_expert, item_lo, item_hi = _routing_tables(cnt[:, 0], idx, n_items)
        yp = _experts(item_tile, item_expert, item_lo, item_hi, tokens, pairs, xn, w_gate, w_up, w_down,
                      b_gate4, b_up4, b_down4, l)
        h = _combine(h, wts.T, norm_final.reshape(1, d), yp.reshape(TOP_K, tp, d), tm_moe, l == depth - 1)

    return h.reshape(bsz, lp, d)[:, N_META:l_real]
```

```python
import functools
import math

import jax
import jax.numpy as jnp
from jax import lax
from jax.experimental import pallas as pl
from jax.experimental.pallas import tpu as pltpu

F32 = jnp.float32
BF16 = jnp.bfloat16
I32 = jnp.int32

N_META = 16
HEAD_DIM = 64
N_HEADS = 8
D_ATTN = N_HEADS * HEAD_DIM
D_SSM = 512
SSM_GROUP = 16
SSM_STATE = 64
N_GROUPS = D_SSM // SSM_GROUP
N_EXPERTS = 32
TOP_K = 4
SWIGLU_LIMIT = 7.0
SWIGLU_ALPHA = 1.702
RMS_EPS = 1e-6

LANES = 128
SEQ_ALIGN = 256
SSM_CHUNK = 16
GROUPS_PER_BUNDLE = LANES // SSM_GROUP
N_BUNDLES = N_GROUPS // GROUPS_PER_BUNDLE
BUNDLE_IN = SSM_CHUNK * LANES
BUNDLE_STATE = 2 * GROUPS_PER_BUNDLE * SSM_STATE
EXPERT_TILE = 256
VMEM_LIMIT = 56 * 1024 * 1024
NEG_BIG = -1e30

_NT = (((1,), (1,)), ((), ()))


def _cparams(*sem):
    return pltpu.CompilerParams(dimension_semantics=sem, vmem_limit_bytes=VMEM_LIMIT)


def _rms(x, g):
    return x * lax.rsqrt(jnp.mean(x * x, axis=-1, keepdims=True) + RMS_EPS) * g


def _sigmoid(x):
    return 1.0 / (1.0 + jnp.exp(-x))


N_SPLIT = 3
AUG_ROWS = LANES - HEAD_DIM


def _split3(x):
    x1 = x.astype(BF16)
    r1 = x - x1.astype(F32)
    x2 = r1.astype(BF16)
    x3 = (r1 - x2.astype(F32)).astype(BF16)
    return x1, x2, x3


def _in_proj_kernel(h_ref, g_ref, wa_ref, wbt_ref, bf_ref, tri_ref, place_ref,
                    u_ref, ka_ref, qa_ref, vt_ref, carry_ref, stage_ref):
    j = pl.program_id(1)
    tm = h_ref.shape[0]

    @pl.when(j == 0)
    def _():
        carry_ref[...] = jnp.zeros_like(carry_ref)

    xb = _rms(h_ref[...], g_ref[...]).astype(BF16)
    pa = jnp.dot(xb, wa_ref[...], preferred_element_type=F32)
    for gb in range(N_BUNDLES):
        stage_ref[...] = pa[:, gb * LANES:(gb + 1) * LANES]
        for t in range(SSM_CHUNK):
            u_ref[gb, :, t * LANES:(t + 1) * LANES] = stage_ref[
                pl.ds(t, tm // SSM_CHUNK, stride=SSM_CHUNK), :].astype(BF16)
    o_k = D_SSM
    o_f = D_SSM + N_HEADS * LANES

    z = pa[:, o_f:o_f + LANES] + bf_ref[...]
    lf = jnp.minimum(z, 0.0) - jnp.log(1.0 + jnp.exp(-jnp.abs(z)))
    tri = tri_ref[...]
    cs = carry_ref[...]
    for piece in _split3(lf):
        cs = cs + jnp.dot(tri, piece, preferred_element_type=F32)
    carry_ref[...] = cs[tm - 1:tm, :]
    aug = jnp.dot(jnp.concatenate(_split3(cs), axis=1), place_ref[...], preferred_element_type=F32)
    ka_ref[...] = (pa[:, o_k:o_f] + aug).astype(BF16)

    pbt = lax.dot_general(wbt_ref[...], xb, _NT, preferred_element_type=F32)
    row = lax.broadcasted_iota(I32, (AUG_ROWS, tm), 0)
    ones_rows = jnp.where(row < N_SPLIT, 1.0, 0.0).astype(BF16)
    for hd in range(N_HEADS):
        qa_ref[hd * LANES:hd * LANES + HEAD_DIM, :] = (
            pbt[hd * HEAD_DIM:(hd + 1) * HEAD_DIM, :] * (HEAD_DIM ** -0.5)).astype(BF16)
        qa_ref[hd * LANES + HEAD_DIM:(hd + 1) * LANES, :] = ones_rows
    tk = vt_ref.shape[2]
    for blk in range(tm // tk):
        vt_ref[blk] = pbt[D_ATTN:, blk * tk:(blk + 1) * tk].astype(BF16)


def _in_proj(h, g, wa, wbt, bf, tm, tk):
    bsz, lp, d = h.shape
    tri = (jnp.arange(tm)[:, None] >= jnp.arange(tm)[None, :]).astype(BF16)
    src = jnp.arange(N_SPLIT * LANES)
    dst = (src % LANES) * LANES + HEAD_DIM + src // LANES
    place = jnp.where((src % LANES < N_HEADS)[:, None] & (dst[:, None] == jnp.arange(N_HEADS * LANES)[None, :]),
                      -1.0, 0.0).astype(BF16)
    full = lambda a: pl.BlockSpec(a.shape, lambda b, j: (0,) * a.ndim)
    return pl.pallas_call(
        _in_proj_kernel,
        grid=(bsz, lp // tm),
        in_specs=[
            pl.BlockSpec((None, tm, d), lambda b, j: (b, j, 0)),
            full(g), full(wa), full(wbt), full(bf), full(tri), full(place),
        ],
        out_specs=[
            pl.BlockSpec((N_BUNDLES, None, tm // SSM_CHUNK, BUNDLE_IN), lambda b, j: (0, b, j, 0)),
            pl.BlockSpec((None, tm, N_HEADS * LANES), lambda b, j: (b, j, 0)),
            pl.BlockSpec((None, N_HEADS * LANES, tm), lambda b, j: (b, 0, j)),
            pl.BlockSpec((None, tm // tk, D_ATTN, tk), lambda b, j: (b, j, 0, 0)),
        ],
        out_shape=[
            jax.ShapeDtypeStruct((N_BUNDLES, bsz, lp // SSM_CHUNK, BUNDLE_IN), BF16),
            jax.ShapeDtypeStruct((bsz, lp, N_HEADS * LANES), BF16),
            jax.ShapeDtypeStruct((bsz, N_HEADS * LANES, lp), BF16),
            jax.ShapeDtypeStruct((bsz, lp // tk, D_ATTN, tk), BF16),
        ],
        scratch_shapes=[pltpu.VMEM((1, LANES), F32), pltpu.VMEM((tm, LANES), F32)],
        compiler_params=_cparams("parallel", "arbitrary"),
        name="in_proj",
    )(h, g, wa, wbt, bf, tri, place)


def _ssm_kernel(x_ref, wt_ref, wb_ref, wc_ref, a_ref, o_ref, z_ref, s_ref):
    x = x_ref[...]
    nc = x.shape[0]
    half = BUNDLE_STATE // 2
    z_ref[...] = jnp.dot(x, wb_ref[...], preferred_element_type=F32)
    a_re = a_ref[:, :half]
    a_im = a_ref[:, half:]

    def step(c8, carry):
        s_re, s_im = carry
        start = pl.multiple_of(c8 * 8, 8)
        zblk = z_ref[pl.ds(start, 8), :]
        rows = []
        for r in range(8):
            rows.append(jnp.concatenate([s_re, s_im], axis=1))
            z_re = zblk[r:r + 1, :half]
            z_im = zblk[r:r + 1, half:]
            s_re, s_im = (a_re * s_re - a_im * s_im + z_re,
                          a_re * s_im + a_im * s_re + z_im)
        s_ref[pl.ds(start, 8), :] = jnp.concatenate(rows, axis=0)
        return s_re, s_im

    zero = jnp.zeros((1, half), F32)
    lax.fori_loop(0, nc // 8, step, (zero, zero))
    y = jnp.dot(x, wt_ref[...], preferred_element_type=F32)
    y = y + jnp.dot(s_ref[...].astype(BF16), wc_ref[...], preferred_element_type=F32)
    for t in range(SSM_CHUNK):
        o_ref[pl.ds(t, nc, stride=SSM_CHUNK), :] = y[:, t * LANES:(t + 1) * LANES]


def _ssm(xb, wt, wb, wc, a16):
    nb, bsz, nc, _ = xb.shape
    return pl.pallas_call(
        _ssm_kernel,
        grid=(nb, bsz),
        in_specs=[
            pl.BlockSpec((None, None, nc, BUNDLE_IN), lambda g, b: (g, b, 0, 0)),
            pl.BlockSpec((None, BUNDLE_IN, BUNDLE_IN), lambda g, b: (g, 0, 0)),
            pl.BlockSpec((None, BUNDLE_IN, BUNDLE_STATE), lambda g, b: (g, 0, 0)),
            pl.BlockSpec((None, BUNDLE_STATE, BUNDLE_IN), lambda g, b: (g, 0, 0)),
            pl.BlockSpec((None, 1, BUNDLE_STATE), lambda g, b: (g, 0, 0)),
        ],
        out_specs=pl.BlockSpec((None, None, nc * SSM_CHUNK, LANES), lambda g, b: (g, b, 0, 0)),
        out_shape=jax.ShapeDtypeStruct((nb, bsz, nc * SSM_CHUNK, LANES), F32),
        scratch_shapes=[pltpu.VMEM((nc, BUNDLE_STATE), F32), pltpu.VMEM((nc, BUNDLE_STATE), F32)],
        compiler_params=_cparams("parallel", "parallel"),
        name="s5_chunked",
    )(xb, wt, wb, wc, a16)


def _ssm_operators(lam_re, lam_im, log_dt, b_re, b_im, c_re, c_im, d):
    g_, p_, h_, q_ = N_GROUPS, SSM_STATE, SSM_GROUP, SSM_CHUNK
    nb, g8 = N_BUNDLES, GROUPS_PER_BUNDLE
    hi = lax.Precision.HIGHEST
    lr, li = lam_re.astype(F32), lam_im.astype(F32)
    dt = jnp.exp(log_dt.astype(F32))[:, None]
    steps = jnp.arange(q_ + 1, dtype=F32)[:, None, None]
    mag = jnp.exp((lr * dt)[None] * steps)
    ang = (li * dt)[None] * steps
    pw_re, pw_im = mag * jnp.cos(ang), mag * jnp.sin(ang)
    n_re, n_im = pw_re[1] - 1.0, pw_im[1]
    den = lr * lr + li * li
    z_re, z_im = (n_re * lr + n_im * li) / den, (n_im * lr - n_re * li) / den
    br, bi = b_re.astype(F32), b_im.astype(F32)
    bb_re = z_re[:, :, None] * br - z_im[:, :, None] * bi
    bb_im = z_re[:, :, None] * bi + z_im[:, :, None] * br
    cr, ci = c_re.astype(F32), c_im.astype(F32)

    def blockdiag(x):
        r, c = x.shape[-2:]
        tile = jnp.tile(jnp.eye(c, dtype=F32), (1, g8))
        wide = jnp.einsum("...rc,cy->...ry", x.reshape(*x.shape[:-3], g8 * r, c), tile, precision=hi)
        same = (jnp.arange(g8 * r)[:, None] // r) == (jnp.arange(g8 * c)[None, :] // c)
        return jnp.where(same, wide, 0.0)

    cl_re = cr[None] * pw_re[:q_, :, None, :] - ci[None] * pw_im[:q_, :, None, :]
    cl_im = cr[None] * pw_im[:q_, :, None, :] + ci[None] * pw_re[:q_, :, None, :]
    kd = (jnp.einsum("kgep,gph->kghe", cl_re, bb_re, precision=hi)
          - jnp.einsum("kgep,gph->kghe", cl_im, bb_im, precision=hi))
    skip = jnp.eye(h_, dtype=F32)[None] * d.astype(F32).reshape(g_, h_, 1)
    kd = kd.at[0].add(skip)
    blocks = blockdiag(kd.reshape(q_, nb, g8, h_, h_)).astype(BF16)
    zero = jnp.zeros_like(blocks[:1])
    wt = jnp.stack([jnp.concatenate([zero] * t + [blocks[:q_ - t]], axis=0) for t in range(q_)])
    wt = wt.transpose(2, 0, 3, 1, 4).reshape(nb, BUNDLE_IN, BUNDLE_IN)

    rev_re, rev_im = pw_re[q_ - 1::-1][:q_], pw_im[q_ - 1::-1][:q_]
    bp_re = rev_re[:, :, :, None] * bb_re[None] - rev_im[:, :, :, None] * bb_im[None]
    bp_im = rev_re[:, :, :, None] * bb_im[None] + rev_im[:, :, :, None] * bb_re[None]
    to_hp = lambda x: x.transpose(0, 1, 3, 2).reshape(q_, nb, g8, h_, p_)
    wb = jnp.concatenate([blockdiag(to_hp(bp_re)), blockdiag(to_hp(bp_im))], axis=-1).astype(BF16)
    wb = wb.transpose(1, 0, 2, 3).reshape(nb, BUNDLE_IN, BUNDLE_STATE)

    cq_re = cr[None] * pw_re[1:, :, None, :] - ci[None] * pw_im[1:, :, None, :]
    cq_im = cr[None] * pw_im[1:, :, None, :] + ci[None] * pw_re[1:, :, None, :]
    to_ph = lambda x: x.transpose(0, 1, 3, 2).reshape(q_, nb, g8, p_, h_)
    wc = jnp.concatenate([blockdiag(to_ph(cq_re)), blockdiag(to_ph(-cq_im))], axis=-2).astype(BF16)
    wc = wc.transpose(1, 2, 0, 3).reshape(nb, BUNDLE_STATE, BUNDLE_IN)

    a16 = jnp.stack([pw_re[q_], pw_im[q_]]).reshape(2, nb, g8 * p_)
    a16 = a16.transpose(1, 0, 2).reshape(nb, 1, BUNDLE_STATE)
    return wt, wb, wc, a16


def _attn_kernel(qa_ref, ka_ref, vt_ref, g_ref, o_ref):
    tq = qa_ref.shape[1]
    tk = vt_ref.shape[2]
    i = pl.program_id(1)
    causal = lax.broadcasted_iota(I32, (tk, tq), 0) <= lax.broadcasted_iota(I32, (tk, tq), 1)

    def block(j, carry, masked):
        start = pl.multiple_of(j * tk, tk)
        scores = []
        for hd in range(N_HEADS):
            kb = ka_ref[pl.ds(start, tk), hd * LANES:(hd + 1) * LANES]
            scores.append(jnp.dot(kb, qa_ref[hd * LANES:(hd + 1) * LANES, :], preferred_element_type=F32))
        stats = []
        for hd in range(N_HEADS):
            m_old, l_old = carry[3 * hd:3 * hd + 2]
            s = scores[hd]
            if masked:
                s = jnp.where(causal, s, NEG_BIG)
            m_new = jnp.maximum(m_old, jnp.max(s, axis=0, keepdims=True))
            pr = jnp.exp(s - m_new)
            alpha = jnp.exp(m_old - m_new)
            l_new = alpha * l_old + jnp.sum(pr, axis=0, keepdims=True)
            stats.append((m_new, l_new, alpha, pr.astype(BF16)))
        new = []
        for hd in range(N_HEADS):
            m_new, l_new, alpha, pb = stats[hd]
            vb = vt_ref[j, hd * HEAD_DIM:(hd + 1) * HEAD_DIM, :]
            acc = alpha * carry[3 * hd + 2] + jnp.dot(vb, pb, preferred_element_type=F32)
            new += [m_new, l_new, acc]
        return tuple(new)

    init = (jnp.full((1, tq), NEG_BIG, F32), jnp.zeros((1, tq), F32), jnp.zeros((HEAD_DIM, tq), F32)) * N_HEADS
    carry = lax.fori_loop(0, i, functools.partial(block, masked=False), init)
    carry = block(i, carry, True)
    outs = [carry[3 * hd + 2] / carry[3 * hd + 1] for hd in range(N_HEADS)]
    y = jnp.concatenate(outs, axis=0).T
    o_ref[...] = _rms(y, g_ref[...]).astype(o_ref.dtype)


def _attention(qa, ka, vt, g, tq):
    bsz, lp, width = ka.shape
    nkv, _, tk = vt.shape[1:]
    assert tq == tk
    return pl.pallas_call(
        _attn_kernel,
        grid=(bsz, lp // tq),
        in_specs=[
            pl.BlockSpec((None, width, tq), lambda b, i: (b, 0, i)),
            pl.BlockSpec((None, lp, width), lambda b, i: (b, 0, 0)),
            pl.BlockSpec((None, nkv, D_ATTN, tk), lambda b, i: (b, 0, 0, 0)),
            pl.BlockSpec((1, D_ATTN), lambda b, i: (0, 0)),
        ],
        out_specs=pl.BlockSpec((None, tq, D_ATTN), lambda b, i: (b, i, 0)),
        out_shape=jax.ShapeDtypeStruct((bsz, lp, D_ATTN), BF16),
        compiler_params=_cparams("parallel", "parallel"),
        name="fox_attention",
    )(qa, ka, vt, g)


def _mix_kernel(h_ref, ys_ref, ya_ref, wglu_ref, bglu_ref, gs_ref, wout_ref, gf_ref,
                wrh_ref, wrl_ref, br_ref, real_ref,
                ho_ref, xn_ref, idx_ref, wts_ref, cnt_ref, carry_ref):
    i = pl.program_id(0)
    tm = h_ref.shape[0]

    @pl.when(i == 0)
    def _():
        carry_ref[...] = jnp.zeros_like(carry_ref)

    ysm = jnp.concatenate([ys_ref[gb].astype(F32) for gb in range(N_BUNDLES)], axis=1)
    c0 = math.sqrt(2.0 / math.pi)
    gl = 0.5 * ysm * (1.0 + jnp.tanh(c0 * (ysm + 0.044715 * (ysm * ysm * ysm))))
    z = jnp.dot(gl.astype(BF16), wglu_ref[...], preferred_element_type=F32) + bglu_ref[...]
    y2 = gl * _sigmoid(z)
    mixed = jnp.concatenate([_rms(y2, gs_ref[...]).astype(BF16), ya_ref[...]], axis=1)
    hn = h_ref[...] + jnp.dot(mixed, wout_ref[...], preferred_element_type=F32)
    ho_ref[...] = hn
    xn = _rms(hn, gf_ref[...])
    xn_ref[...] = xn

    xh = xn.astype(BF16)
    xl = (xn - xh.astype(F32)).astype(BF16)
    wrh = wrh_ref[...]
    lg = (lax.dot_general(wrh, xh, _NT, preferred_element_type=F32)
          + lax.dot_general(wrh, xl, _NT, preferred_element_type=F32)
          + lax.dot_general(wrl_ref[...], xh, _NT, preferred_element_type=F32)) + br_ref[...]

    e_iota = lax.broadcasted_iota(I32, (N_EXPERTS, tm), 0)
    work = lg
    sel = jnp.zeros((N_EXPERTS, tm), jnp.bool_)
    top_i, top_v = [], []
    for _ in range(TOP_K):
        m = jnp.max(work, axis=0, keepdims=True)
        ik = jnp.min(jnp.where(work == m, e_iota, N_EXPERTS), axis=0, keepdims=True)
        hit = e_iota == ik
        sel = jnp.logical_or(sel, hit)
        work = jnp.where(hit, -jnp.inf, work)
        top_i.append(ik)
        top_v.append(m)
    tv = jnp.concatenate(top_v, axis=0)
    ew = jnp.exp(tv - tv[0:1])
    wts_ref[...] = ew / jnp.sum(ew, axis=0, keepdims=True)
    idx_ref[...] = jnp.concatenate(top_i, axis=0)

    sel = jnp.logical_and(sel, real_ref[...] > 0)
    carry_ref[...] = carry_ref[...] + jnp.sum(sel.astype(F32), axis=1, keepdims=True)
    cnt_ref[...] = carry_ref[...]


def _mix_route(h, ys, ya, wglu, bglu, gs, wout, gf, wrh, wrl, br, real, tm):
    tp, d = h.shape
    full = lambda shape: pl.BlockSpec(shape, lambda i: (0,) * len(shape))
    return pl.pallas_call(
        _mix_kernel,
        grid=(tp // tm,),
        in_specs=[
            pl.BlockSpec((tm, d), lambda i: (i, 0)),
            pl.BlockSpec((N_BUNDLES, tm, LANES), lambda i: (0, i, 0)),
            pl.BlockSpec((tm, D_ATTN), lambda i: (i, 0)),
            full((D_SSM, D_SSM)), full((1, D_SSM)), full((1, D_SSM)),
            full((d, d)), full((1, d)),
            full((N_EXPERTS, d)), full((N_EXPERTS, d)), full((N_EXPERTS, 1)),
            pl.BlockSpec((1, tm), lambda i: (0, i)),
        ],
        out_specs=[
            pl.BlockSpec((tm, d), lambda i: (i, 0)),
            pl.BlockSpec((tm, d), lambda i: (i, 0)),
            pl.BlockSpec((TOP_K, tm), lambda i: (0, i)),
            pl.BlockSpec((TOP_K, tm), lambda i: (0, i)),
            pl.BlockSpec((N_EXPERTS, LANES), lambda i: (0, 0)),
        ],
        out_shape=[
            jax.ShapeDtypeStruct((tp, d), F32),
            jax.ShapeDtypeStruct((tp, d), F32),
            jax.ShapeDtypeStruct((TOP_K, tp), I32),
            jax.ShapeDtypeStruct((TOP_K, tp), F32),
            jax.ShapeDtypeStruct((N_EXPERTS, LANES), F32),
        ],
        scratch_shapes=[pltpu.VMEM((N_EXPERTS, LANES), F32)],
        compiler_params=_cparams("arbitrary"),
        name="mix_route",
    )(h, ys, ya, wglu, bglu, gs, wout, gf, wrh, wrl, br, real)


def _row_dma(src_ref, src_row, dst_ref, dst_row, sem):
    return pltpu.make_async_copy(src_ref.at[pl.ds(src_row, 1), :], dst_ref.at[pl.ds(dst_row, 1), :], sem)


def _expert_kernel(wt_ref, we_ref, lo_ref, hi_ref, tok_ref, tok_next_ref, pair_ref, pair_prev_ref,
                   wg_ref, wu_ref, wd_ref, bg_ref, bu_ref, bd_ref, xn_ref,
                   yp_ref, wgb_ref, wub_ref, wdb_ref, xbuf_ref, ybuf_ref, gsem, ssem, *, pad_blocks):
    w = pl.program_id(0)
    last_w = pl.num_programs(0) - 1
    tile = wt_ref[w]
    slot = lax.rem(tile, 2)
    other = 1 - slot
    prev = jnp.maximum(w - 1, 0)
    lo, hi = lo_ref[w], hi_ref[w]
    new_expert = jnp.logical_or(w == 0, we_ref[w] != we_ref[prev])
    first = jnp.logical_or(w == 0, wt_ref[prev] != tile)
    rows = range(EXPERT_TILE)

    def gather(tokens_ref, s):
        for r in rows:
            _row_dma(xn_ref, tokens_ref[0, r], xbuf_ref.at[s], r, gsem.at[s]).start(priority=r % 2)

    def scatter(s, pairs_ref):
        for r in rows:
            _row_dma(ybuf_ref.at[s], r, yp_ref, pairs_ref[0, r], ssem.at[s]).start(priority=r % 2)

    def wait_gather(s):
        for r in rows:
            _row_dma(xn_ref, 0, xbuf_ref.at[s], 0, gsem.at[s]).wait()

    def wait_scatter(s):
        for r in rows:
            _row_dma(ybuf_ref.at[s], 0, yp_ref, 0, ssem.at[s]).wait()

    @pl.when(w == 0)
    def _():
        gather(tok_ref, slot)
        ybuf_ref[...] = jnp.zeros_like(ybuf_ref)
        zero_fills = [pltpu.make_async_copy(ybuf_ref.at[other, pl.ds(0, n), :], yp_ref.at[pl.ds(start, n), :],
                                            ssem.at[other]) for start, n in pad_blocks]
        for fill in zero_fills:
            fill.start()
        for fill in zero_fills:
            fill.wait()
        scatter(slot, pair_ref)

    @pl.when(new_expert)
    def _():
        wgb_ref[...] = wg_ref[...].astype(BF16)
        wub_ref[...] = wu_ref[...].astype(BF16)
        wdb_ref[...] = wd_ref[...].astype(BF16)

    def item(is_first):
        if is_first:
            wait_gather(slot)
            wait_scatter(slot)
            scatter(other, pair_prev_ref)
            gather(tok_next_ref, other)
        x = xbuf_ref[slot].astype(BF16)
        g = jnp.minimum(jnp.dot(x, wgb_ref[...], preferred_element_type=F32) + bg_ref[...], SWIGLU_LIMIT)
        u = jnp.clip(jnp.dot(x, wub_ref[...], preferred_element_type=F32) + bu_ref[...],
                     -SWIGLU_LIMIT, SWIGLU_LIMIT)
        hm = (g * _sigmoid(SWIGLU_ALPHA * g) * (u + 1.0)).astype(BF16)
        y = jnp.dot(hm, wdb_ref[...], preferred_element_type=F32) + bd_ref[...]
        row = lax.broadcasted_iota(I32, (EXPERT_TILE, 1), 0)
        mine = jnp.logical_and(row >= lo, row < hi)
        ybuf_ref[slot] = jnp.where(mine, y, 0.0 if is_first else ybuf_ref[slot])

    pl.when(first)(functools.partial(item, True))
    pl.when(jnp.logical_not(first))(functools.partial(item, False))

    @pl.when(w == last_w)
    def _():
        scatter(slot, pair_ref)
        wait_scatter(slot)
        wait_scatter(other)
        wait_gather(other)


def _experts(item_tile, item_expert, item_lo, item_hi, tokens, pairs, xn, wg, wu, wd, bg, bu, bd, layer,
             n_pairs, pad_blocks):
    n_tiles = tokens.shape[0]
    assert all(n <= EXPERT_TILE for _, n in pad_blocks)
    d = xn.shape[1]
    dff = wg.shape[3]
    wspec = lambda r, c: pl.BlockSpec((None, None, r, c), lambda w, wt, we, lo, hi: (layer, we[w], 0, 0))
    ids = lambda shift: pl.BlockSpec(
        (None, 1, EXPERT_TILE), lambda w, wt, we, lo, hi: (jnp.clip(wt[w] + shift, 0, n_tiles - 1), 0, 0),
        memory_space=pltpu.SMEM)
    return pl.pallas_call(
        functools.partial(_expert_kernel, pad_blocks=pad_blocks),
        grid_spec=pltpu.PrefetchScalarGridSpec(
            num_scalar_prefetch=4,
            grid=(item_tile.shape[0],),
            in_specs=[
                ids(0), ids(1), ids(0), ids(-1),
                wspec(d, dff), wspec(d, dff), wspec(dff, d),
                wspec(1, dff), wspec(1, dff), wspec(1, d),
                pl.BlockSpec(memory_space=pl.ANY),
            ],
            out_specs=pl.BlockSpec(memory_space=pl.ANY),
            scratch_shapes=[pltpu.VMEM((d, dff), BF16), pltpu.VMEM((d, dff), BF16), pltpu.VMEM((dff, d), BF16),
                            pltpu.VMEM((2, EXPERT_TILE, d), F32), pltpu.VMEM((2, EXPERT_TILE, d), F32),
                            pltpu.SemaphoreType.DMA((2,)), pltpu.SemaphoreType.DMA((2,))],
        ),
        out_shape=jax.ShapeDtypeStruct((n_pairs, d), F32),
        compiler_params=_cparams("arbitrary"),
        name="moe_experts",
    )(item_tile, item_expert, item_lo, item_hi, tokens, tokens, pairs, pairs, wg, wu, wd, bg, bu, bd, xn)


def _combine_kernel(h_ref, w_ref, g_ref, yp_ref, o_ref, *, final_norm):
    out = h_ref[...]
    for k in range(TOP_K):
        out = out + w_ref[:, k:k + 1] * yp_ref[k]
    if final_norm:
        out = _rms(out, g_ref[...])
    o_ref[...] = out


def _combine(h, wcol, g, yp, tm, final_norm):
    tp, d = h.shape
    return pl.pallas_call(
        functools.partial(_combine_kernel, final_norm=final_norm),
        grid=(tp // tm,),
        in_specs=[
            pl.BlockSpec((tm, d), lambda i: (i, 0)),
            pl.BlockSpec((tm, TOP_K), lambda i: (i, 0)),
            pl.BlockSpec((1, d), lambda i: (0, 0)),
            pl.BlockSpec((TOP_K, tm, d), lambda i: (0, i, 0)),
        ],
        out_specs=pl.BlockSpec((tm, d), lambda i: (i, 0)),
        out_shape=jax.ShapeDtypeStruct((tp, d), F32),
        compiler_params=_cparams("parallel"),
        name="moe_combine",
    )(h, wcol, g, yp)


def _routing_tables(cnt, idx, real, n_slots, n_items):
    cnt = cnt.astype(I32)
    end = jnp.cumsum(cnt)
    offs = end - cnt
    experts = jnp.arange(N_EXPERTS, dtype=I32)

    def lookup(table, e):
        hit = e[None] == experts.reshape((N_EXPERTS,) + (1,) * e.ndim)
        return jnp.sum(jnp.where(hit, table.reshape(hit.shape[:1] + (1,) * e.ndim), 0), axis=0)

    tp = idx.shape[1]
    n_pairs = TOP_K * tp
    group = jnp.where(jnp.tile(real.reshape(tp), TOP_K) > 0, idx.reshape(n_pairs), N_EXPERTS)
    pair = jnp.sort(group * n_pairs + jnp.arange(n_pairs, dtype=I32))[:n_slots] % n_pairs
    pairs = pair.reshape(n_slots // EXPERT_TILE, 1, EXPERT_TILE)
    tokens = pairs % tp
    first_tile = offs // EXPERT_TILE
    n_e = jnp.where(cnt > 0, (end - 1) // EXPERT_TILE - first_tile + 1, 0)
    cum_items = jnp.cumsum(n_e)
    w = jnp.minimum(jnp.arange(n_items, dtype=I32), cum_items[-1] - 1)
    e_w = jnp.minimum(jnp.sum((w[None] >= cum_items[:, None]).astype(I32), axis=0), N_EXPERTS - 1)
    tile_w = lookup(first_tile, e_w) + (w - lookup(cum_items - n_e, e_w))
    base = tile_w * EXPERT_TILE
    lo_w = jnp.maximum(lookup(offs, e_w), base) - base
    hi_w = jnp.minimum(lookup(end, e_w), base + EXPERT_TILE) - base
    hi_w = jnp.where(jnp.arange(n_items) < cum_items[-1], hi_w, lo_w)
    return tokens, pairs, tile_w.astype(I32), e_w, lo_w.astype(I32), hi_w.astype(I32)


def kernel(x, meta_tokens, norm_mix, w_in, ssm_lambda_re, ssm_lambda_im, ssm_log_dt, ssm_b_re, ssm_b_im, ssm_c_re, ssm_c_im, ssm_d, w_glu, b_glu, b_forget, norm_ssm_out, norm_attn_out, w_out, norm_ffn, w_router, b_router, w_gate, b_gate, w_up, b_up, w_down, b_down, norm_final):
    bsz, seq, d = x.shape
    depth = w_in.shape[0]
    l_real = N_META + seq
    lp = -(-l_real // SEQ_ALIGN) * SEQ_ALIGN
    tp = bsz * lp
    nc = lp // SSM_CHUNK
    tm_proj = 512 if tp % 512 == 0 else 256
    tm_moe = 256
    n_pairs = TOP_K * tp
    n_slots = -(-(TOP_K * bsz * l_real) // EXPERT_TILE) * EXPERT_TILE
    n_items = n_slots // EXPERT_TILE + N_EXPERTS
    o_q, o_k, o_v, o_f = D_SSM, D_SSM + D_ATTN, D_SSM + 2 * D_ATTN, D_SSM + 3 * D_ATTN
    real = (jnp.arange(tp, dtype=I32) % lp < l_real).astype(I32).reshape(1, tp)
    pad_blocks = tuple((k * tp + b * lp + l_real, lp - l_real)
                       for k in range(TOP_K) for b in range(bsz) if lp > l_real)

    meta = jnp.broadcast_to(meta_tokens[None].astype(x.dtype), (bsz, N_META, d))
    pad = jnp.zeros((bsz, lp - l_real, d), x.dtype)
    h = jnp.concatenate([meta, x, pad], axis=1).reshape(tp, d)
    expert_bias = lambda b: b.reshape(depth, N_EXPERTS, 1, b.shape[-1])
    b_gate4, b_up4, b_down4 = expert_bias(b_gate), expert_bias(b_up), expert_bias(b_down)

    for l in range(depth):
        w_l = w_in[l]
        w_k = jnp.pad(w_l[:, o_k:o_v].reshape(d, N_HEADS, HEAD_DIM), ((0, 0), (0, 0), (0, AUG_ROWS)))
        w_a = jnp.concatenate([w_l[:, :o_q], w_k.reshape(d, N_HEADS * LANES),
                               jnp.pad(w_l[:, o_f:], ((0, 0), (0, LANES - N_HEADS)))], axis=1).astype(BF16)
        w_bt = jnp.concatenate([w_l[:, o_q:o_k], w_l[:, o_v:o_f]], axis=1).T.astype(BF16)
        bf = jnp.pad(b_forget[l].astype(F32), (0, LANES - N_HEADS)).reshape(1, LANES)
        u, ka, qa, vt = _in_proj(h.reshape(bsz, lp, d), norm_mix[l].reshape(1, d), w_a, w_bt, bf,
                                 SEQ_ALIGN, SEQ_ALIGN)

        wt, wb, wc, a16 = _ssm_operators(ssm_lambda_re[l], ssm_lambda_im[l], ssm_log_dt[l], ssm_b_re[l],
                                         ssm_b_im[l], ssm_c_re[l], ssm_c_im[l], ssm_d[l])
        ys = _ssm(u, wt, wb, wc, a16).reshape(N_BUNDLES, tp, LANES)

        ya = _attention(qa, ka, vt, norm_attn_out[l].reshape(1, D_ATTN), SEQ_ALIGN).reshape(tp, D_ATTN)

        wr = w_router[l].T.astype(F32)
        wrh = wr.astype(BF16)
        wrl = (wr - wrh.astype(F32)).astype(BF16)
        h, xn, idx, wts, cnt = _mix_route(
            h, ys, ya, w_glu[l].astype(BF16), b_glu[l].reshape(1, D_SSM), norm_ssm_out[l].reshape(1, D_SSM),
            w_out[l].astype(BF16), norm_ffn[l].reshape(1, d), wrh, wrl,
            b_router[l].astype(F32).reshape(N_EXPERTS, 1), real, tm_proj)

        tokens, pairs, item_tile, item_expert, item_lo, item_hi = _routing_tables(
            cnt[:, 0], idx, real, n_slots, n_items)
        yp = _experts(item_tile, item_expert, item_lo, item_hi, tokens, pairs, xn, w_gate, w_up, w_down,
                      b_gate4, b_up4, b_down4, l, n_pairs, pad_blocks)
        h = _combine(h, wts.T, norm_final.reshape(1, d), yp.reshape(TOP_K, tp, d), tm_moe, l == depth - 1)

    return h.reshape(bsz, lp, d)[:, N_META:l_real]
```

```python
import functools
import math

import jax
import jax.numpy as jnp
from jax import lax
from jax.experimental import pallas as pl
from jax.experimental.pallas import tpu as pltpu

F32 = jnp.float32
BF16 = jnp.bfloat16
I32 = jnp.int32

N_META = 16
HEAD_DIM = 64
N_HEADS = 8
D_ATTN = N_HEADS * HEAD_DIM
D_SSM = 512
SSM_GROUP = 16
SSM_STATE = 64
N_GROUPS = D_SSM // SSM_GROUP
N_EXPERTS = 32
TOP_K = 4
SWIGLU_LIMIT = 7.0
SWIGLU_ALPHA = 1.702
RMS_EPS = 1e-6

LANES = 128
SEQ_ALIGN = 256
SSM_CHUNK = 16
GROUPS_PER_BUNDLE = LANES // SSM_GROUP
N_BUNDLES = N_GROUPS // GROUPS_PER_BUNDLE
BUNDLE_IN = SSM_CHUNK * LANES
BUNDLE_STATE = 2 * GROUPS_PER_BUNDLE * SSM_STATE
EXPERT_TILE = 256
VMEM_LIMIT = 56 * 1024 * 1024
NEG_BIG = -1e30

_NT = (((1,), (1,)), ((), ()))


def _cparams(*sem):
    return pltpu.CompilerParams(dimension_semantics=sem, vmem_limit_bytes=VMEM_LIMIT)


def _rms(x, g):
    return x * lax.rsqrt(jnp.mean(x * x, axis=-1, keepdims=True) + RMS_EPS) * g


def _sigmoid(x):
    return 1.0 / (1.0 + jnp.exp(-x))


N_SPLIT = 3
AUG_ROWS = LANES - HEAD_DIM


def _split3(x):
    x1 = x.astype(BF16)
    r1 = x - x1.astype(F32)
    x2 = r1.astype(BF16)
    x3 = (r1 - x2.astype(F32)).astype(BF16)
    return x1, x2, x3


def _in_proj_kernel(h_ref, g_ref, wa_ref, wbt_ref, bf_ref, tri_ref, place_ref,
                    u_ref, ka_ref, qa_ref, vt_ref, carry_ref, stage_ref):
    j = pl.program_id(1)
    tm = h_ref.shape[0]

    @pl.when(j == 0)
    def _():
        carry_ref[...] = jnp.zeros_like(carry_ref)

    xb = _rms(h_ref[...], g_ref[...]).astype(BF16)
    pa = jnp.dot(xb, wa_ref[...], preferred_element_type=F32)
    for gb in range(N_BUNDLES):
        stage_ref[...] = pa[:, gb * LANES:(gb + 1) * LANES]
        for t in range(SSM_CHUNK):
            u_ref[gb, :, t * LANES:(t + 1) * LANES] = stage_ref[
                pl.ds(t, tm // SSM_CHUNK, stride=SSM_CHUNK), :].astype(BF16)
    o_k = D_SSM
    o_f = D_SSM + N_HEADS * LANES

    z = pa[:, o_f:o_f + LANES] + bf_ref[...]
    lf = jnp.minimum(z, 0.0) - jnp.log(1.0 + jnp.exp(-jnp.abs(z)))
    tri = tri_ref[...]
    cs = carry_ref[...]
    for piece in _split3(lf):
        cs = cs + jnp.dot(tri, piece, preferred_element_type=F32)
    carry_ref[...] = cs[tm - 1:tm, :]
    aug = jnp.dot(jnp.concatenate(_split3(cs), axis=1), place_ref[...], preferred_element_type=F32)
    ka_ref[...] = (pa[:, o_k:o_f] + aug).astype(BF16)

    pbt = lax.dot_general(wbt_ref[...], xb, _NT, preferred_element_type=F32)
    row = lax.broadcasted_iota(I32, (AUG_ROWS, tm), 0)
    ones_rows = jnp.where(row < N_SPLIT, 1.0, 0.0).astype(BF16)
    for hd in range(N_HEADS):
        qa_ref[hd * LANES:hd * LANES + HEAD_DIM, :] = (
            pbt[hd * HEAD_DIM:(hd + 1) * HEAD_DIM, :] * (HEAD_DIM ** -0.5)).astype(BF16)
        qa_ref[hd * LANES + HEAD_DIM:(hd + 1) * LANES, :] = ones_rows
    tk = vt_ref.shape[2]
    for blk in range(tm // tk):
        vt_ref[blk] = pbt[D_ATTN:, blk * tk:(blk + 1) * tk].astype(BF16)


def _in_proj(h, g, wa, wbt, bf, tm, tk):
    bsz, lp, d = h.shape
    tri = (jnp.arange(tm)[:, None] >= jnp.arange(tm)[None, :]).astype(BF16)
    src = jnp.arange(N_SPLIT * LANES)
    dst = (src % LANES) * LANES + HEAD_DIM + src // LANES
    place = jnp.where((src % LANES < N_HEADS)[:, None] & (dst[:, None] == jnp.arange(N_HEADS * LANES)[None, :]),
                      -1.0, 0.0).astype(BF16)
    full = lambda a: pl.BlockSpec(a.shape, lambda b, j: (0,) * a.ndim)
    return pl.pallas_call(
        _in_proj_kernel,
        grid=(bsz, lp // tm),
        in_specs=[
            pl.BlockSpec((None, tm, d), lambda b, j: (b, j, 0)),
            full(g), full(wa), full(wbt), full(bf), full(tri), full(place),
        ],
        out_specs=[
            pl.BlockSpec((N_BUNDLES, None, tm // SSM_CHUNK, BUNDLE_IN), lambda b, j: (0, b, j, 0)),
            pl.BlockSpec((None, tm, N_HEADS * LANES), lambda b, j: (b, j, 0)),
            pl.BlockSpec((None, N_HEADS * LANES, tm), lambda b, j: (b, 0, j)),
            pl.BlockSpec((None, tm // tk, D_ATTN, tk), lambda b, j: (b, j, 0, 0)),
        ],
        out_shape=[
            jax.ShapeDtypeStruct((N_BUNDLES, bsz, lp // SSM_CHUNK, BUNDLE_IN), BF16),
            jax.ShapeDtypeStruct((bsz, lp, N_HEADS * LANES), BF16),
            jax.ShapeDtypeStruct((bsz, N_HEADS * LANES, lp), BF16),
            jax.ShapeDtypeStruct((bsz, lp // tk, D_ATTN, tk), BF16),
        ],
        scratch_shapes=[pltpu.VMEM((1, LANES), F32), pltpu.VMEM((tm, LANES), F32)],
        compiler_params=_cparams("parallel", "arbitrary"),
        name="in_proj",
    )(h, g, wa, wbt, bf, tri, place)


def _ssm_kernel(x_ref, wt_ref, wb_ref, wc_ref, a_ref, o_ref, z_ref, s_ref):
    x = x_ref[...]
    nc = x.shape[0]
    half = BUNDLE_STATE // 2
    z_ref[...] = jnp.dot(x, wb_ref[...], preferred_element_type=F32)
    a_re = a_ref[:, :half]
    a_im = a_ref[:, half:]

    def step(c8, carry):
        s_re, s_im = carry
        start = pl.multiple_of(c8 * 8, 8)
        zblk = z_ref[pl.ds(start, 8), :]
        rows = []
        for r in range(8):
            rows.append(jnp.concatenate([s_re, s_im], axis=1))
            z_re = zblk[r:r + 1, :half]
            z_im = zblk[r:r + 1, half:]
            s_re, s_im = (a_re * s_re - a_im * s_im + z_re,
                          a_re * s_im + a_im * s_re + z_im)
        s_ref[pl.ds(start, 8), :] = jnp.concatenate(rows, axis=0)
        return s_re, s_im

    zero = jnp.zeros((1, half), F32)
    lax.fori_loop(0, nc // 8, step, (zero, zero))
    y = jnp.dot(x, wt_ref[...], preferred_element_type=F32)
    y = y + jnp.dot(s_ref[...].astype(BF16), wc_ref[...], preferred_element_type=F32)
    for t in range(SSM_CHUNK):
        o_ref[pl.ds(t, nc, stride=SSM_CHUNK), :] = y[:, t * LANES:(t + 1) * LANES]


def _ssm(xb, wt, wb, wc, a16):
    nb, bsz, nc, _ = xb.shape
    return pl.pallas_call(
        _ssm_kernel,
        grid=(nb, bsz),
        in_specs=[
            pl.BlockSpec((None, None, nc, BUNDLE_IN), lambda g, b: (g, b, 0, 0)),
            pl.BlockSpec((None, BUNDLE_IN, BUNDLE_IN), lambda g, b: (g, 0, 0)),
            pl.BlockSpec((None, BUNDLE_IN, BUNDLE_STATE), lambda g, b: (g, 0, 0)),
            pl.BlockSpec((None, BUNDLE_STATE, BUNDLE_IN), lambda g, b: (g, 0, 0)),
            pl.BlockSpec((None, 1, BUNDLE_STATE), lambda g, b: (g, 0, 0)),
        ],
        out_specs=pl.BlockSpec((None, None, nc * SSM_CHUNK, LANES), lambda g, b: (g, b, 0, 0)),
        out_shape=jax.ShapeDtypeStruct((nb, bsz, nc * SSM_CHUNK, LANES), F32),
        scratch_shapes=[pltpu.VMEM((nc, BUNDLE_STATE), F32), pltpu.VMEM((nc, BUNDLE_STATE), F32)],
        compiler_params=_cparams("parallel", "parallel"),
        name="s5_chunked",
    )(xb, wt, wb, wc, a16)


def _ssm_operators(lam_re, lam_im, log_dt, b_re, b_im, c_re, c_im, d):
    g_, p_, h_, q_ = N_GROUPS, SSM_STATE, SSM_GROUP, SSM_CHUNK
    nb, g8 = N_BUNDLES, GROUPS_PER_BUNDLE
    hi = lax.Precision.HIGHEST
    lr, li = lam_re.astype(F32), lam_im.astype(F32)
    dt = jnp.exp(log_dt.astype(F32))[:, None]
    steps = jnp.arange(q_ + 1, dtype=F32)[:, None, None]
    mag = jnp.exp((lr * dt)[None] * steps)
    ang = (li * dt)[None] * steps
    pw_re, pw_im = mag * jnp.cos(ang), mag * jnp.sin(ang)
    n_re, n_im = pw_re[1] - 1.0, pw_im[1]
    den = lr * lr + li * li
    z_re, z_im = (n_re * lr + n_im * li) / den, (n_im * lr - n_re * li) / den
    br, bi = b_re.astype(F32), b_im.astype(F32)
    bb_re = z_re[:, :, None] * br - z_im[:, :, None] * bi
    bb_im = z_re[:, :, None] * bi + z_im[:, :, None] * br
    cr, ci = c_re.astype(F32), c_im.astype(F32)

    def blockdiag(x):
        r, c = x.shape[-2:]
        tile = jnp.tile(jnp.eye(c, dtype=F32), (1, g8))
        wide = jnp.einsum("...rc,cy->...ry", x.reshape(*x.shape[:-3], g8 * r, c), tile, precision=hi)
        same = (jnp.arange(g8 * r)[:, None] // r) == (jnp.arange(g8 * c)[None, :] // c)
        return jnp.where(same, wide, 0.0)

    cl_re = cr[None] * pw_re[:q_, :, None, :] - ci[None] * pw_im[:q_, :, None, :]
    cl_im = cr[None] * pw_im[:q_, :, None, :] + ci[None] * pw_re[:q_, :, None, :]
    kd = (jnp.einsum("kgep,gph->kghe", cl_re, bb_re, precision=hi)
          - jnp.einsum("kgep,gph->kghe", cl_im, bb_im, precision=hi))
    skip = jnp.eye(h_, dtype=F32)[None] * d.astype(F32).reshape(g_, h_, 1)
    kd = kd.at[0].add(skip)
    blocks = blockdiag(kd.reshape(q_, nb, g8, h_, h_)).astype(BF16)
    zero = jnp.zeros_like(blocks[:1])
    wt = jnp.stack([jnp.concatenate([zero] * t + [blocks[:q_ - t]], axis=0) for t in range(q_)])
    wt = wt.transpose(2, 0, 3, 1, 4).reshape(nb, BUNDLE_IN, BUNDLE_IN)

    rev_re, rev_im = pw_re[q_ - 1::-1][:q_], pw_im[q_ - 1::-1][:q_]
    bp_re = rev_re[:, :, :, None] * bb_re[None] - rev_im[:, :, :, None] * bb_im[None]
    bp_im = rev_re[:, :, :, None] * bb_im[None] + rev_im[:, :, :, None] * bb_re[None]
    to_hp = lambda x: x.transpose(0, 1, 3, 2).reshape(q_, nb, g8, h_, p_)
    wb = jnp.concatenate([blockdiag(to_hp(bp_re)), blockdiag(to_hp(bp_im))], axis=-1).astype(BF16)
    wb = wb.transpose(1, 0, 2, 3).reshape(nb, BUNDLE_IN, BUNDLE_STATE)

    cq_re = cr[None] * pw_re[1:, :, None, :] - ci[None] * pw_im[1:, :, None, :]
    cq_im = cr[None] * pw_im[1:, :, None, :] + ci[None] * pw_re[1:, :, None, :]
    to_ph = lambda x: x.transpose(0, 1, 3, 2).reshape(q_, nb, g8, p_, h_)
    wc = jnp.concatenate([blockdiag(to_ph(cq_re)), blockdiag(to_ph(-cq_im))], axis=-2).astype(BF16)
    wc = wc.transpose(1, 2, 0, 3).reshape(nb, BUNDLE_STATE, BUNDLE_IN)

    a16 = jnp.stack([pw_re[q_], pw_im[q_]]).reshape(2, nb, g8 * p_)
    a16 = a16.transpose(1, 0, 2).reshape(nb, 1, BUNDLE_STATE)
    return wt, wb, wc, a16


def _attn_kernel(qa_ref, ka_ref, vt_ref, g_ref, o_ref):
    tq = qa_ref.shape[1]
    tk = vt_ref.shape[2]
    i = pl.program_id(1)
    causal = lax.broadcasted_iota(I32, (tk, tq), 0) <= lax.broadcasted_iota(I32, (tk, tq), 1)

    def block(j, carry, masked):
        start = pl.multiple_of(j * tk, tk)
        scores = []
        for hd in range(N_HEADS):
            kb = ka_ref[pl.ds(start, tk), hd * LANES:(hd + 1) * LANES]
            scores.append(jnp.dot(kb, qa_ref[hd * LANES:(hd + 1) * LANES, :], preferred_element_type=F32))
        stats = []
        for hd in range(N_HEADS):
            m_old, l_old = carry[3 * hd:3 * hd + 2]
            s = scores[hd]
            if masked:
                s = jnp.where(causal, s, NEG_BIG)
            m_new = jnp.maximum(m_old, jnp.max(s, axis=0, keepdims=True))
            pr = jnp.exp(s - m_new)
            alpha = jnp.exp(m_old - m_new)
            l_new = alpha * l_old + jnp.sum(pr, axis=0, keepdims=True)
            stats.append((m_new, l_new, alpha, pr.astype(BF16)))
        new = []
        for hd in range(N_HEADS):
            m_new, l_new, alpha, pb = stats[hd]
            vb = vt_ref[j, hd * HEAD_DIM:(hd + 1) * HEAD_DIM, :]
            acc = alpha * carry[3 * hd + 2] + jnp.dot(vb, pb, preferred_element_type=F32)
            new += [m_new, l_new, acc]
        return tuple(new)

    init = (jnp.full((1, tq), NEG_BIG, F32), jnp.zeros((1, tq), F32), jnp.zeros((HEAD_DIM, tq), F32)) * N_HEADS
    carry = lax.fori_loop(0, i, functools.partial(block, masked=False), init)
    carry = block(i, carry, True)
    outs = [carry[3 * hd + 2] / carry[3 * hd + 1] for hd in range(N_HEADS)]
    y = jnp.concatenate(outs, axis=0).T
    o_ref[...] = _rms(y, g_ref[...]).astype(o_ref.dtype)


def _attention(qa, ka, vt, g, tq):
    bsz, lp, width = ka.shape
    nkv, _, tk = vt.shape[1:]
    assert tq == tk
    return pl.pallas_call(
        _attn_kernel,
        grid=(bsz, lp // tq),
        in_specs=[
            pl.BlockSpec((None, width, tq), lambda b, i: (b, 0, i)),
            pl.BlockSpec((None, lp, width), lambda b, i: (b, 0, 0)),
            pl.BlockSpec((None, nkv, D_ATTN, tk), lambda b, i: (b, 0, 0, 0)),
            pl.BlockSpec((1, D_ATTN), lambda b, i: (0, 0)),
        ],
        out_specs=pl.BlockSpec((None, tq, D_ATTN), lambda b, i: (b, i, 0)),
        out_shape=jax.ShapeDtypeStruct((bsz, lp, D_ATTN), BF16),
        compiler_params=_cparams("parallel", "parallel"),
        name="fox_attention",
    )(qa, ka, vt, g)


def _mix_kernel(h_ref, ys_ref, ya_ref, wglu_ref, bglu_ref, gs_ref, wout_ref, gf_ref,
                wrh_ref, wrl_ref, br_ref, real_ref,
                ho_ref, xn_ref, idx_ref, wts_ref, cnt_ref, carry_ref):
    i = pl.program_id(0)
    tm = h_ref.shape[0]

    @pl.when(i == 0)
    def _():
        carry_ref[...] = jnp.zeros_like(carry_ref)

    ysm = jnp.concatenate([ys_ref[gb].astype(F32) for gb in range(N_BUNDLES)], axis=1)
    c0 = math.sqrt(2.0 / math.pi)
    gl = 0.5 * ysm * (1.0 + jnp.tanh(c0 * (ysm + 0.044715 * (ysm * ysm * ysm))))
    z = jnp.dot(gl.astype(BF16), wglu_ref[...], preferred_element_type=F32) + bglu_ref[...]
    y2 = gl * _sigmoid(z)
    mixed = jnp.concatenate([_rms(y2, gs_ref[...]).astype(BF16), ya_ref[...]], axis=1)
    hn = h_ref[...] + jnp.dot(mixed, wout_ref[...], preferred_element_type=F32)
    ho_ref[...] = hn
    xn = _rms(hn, gf_ref[...])
    xn_ref[...] = xn

    xh = xn.astype(BF16)
    xl = (xn - xh.astype(F32)).astype(BF16)
    wrh = wrh_ref[...]
    lg = (lax.dot_general(wrh, xh, _NT, preferred_element_type=F32)
          + lax.dot_general(wrh, xl, _NT, preferred_element_type=F32)
          + lax.dot_general(wrl_ref[...], xh, _NT, preferred_element_type=F32)) + br_ref[...]

    e_iota = lax.broadcasted_iota(I32, (N_EXPERTS, tm), 0)
    work = lg
    sel = jnp.zeros((N_EXPERTS, tm), jnp.bool_)
    top_i, top_v = [], []
    for _ in range(TOP_K):
        m = jnp.max(work, axis=0, keepdims=True)
        ik = jnp.min(jnp.where(work == m, e_iota, N_EXPERTS), axis=0, keepdims=True)
        hit = e_iota == ik
        sel = jnp.logical_or(sel, hit)
        work = jnp.where(hit, -jnp.inf, work)
        top_i.append(ik)
        top_v.append(m)
    tv = jnp.concatenate(top_v, axis=0)
    ew = jnp.exp(tv - tv[0:1])
    wts_ref[...] = ew / jnp.sum(ew, axis=0, keepdims=True)
    idx_ref[...] = jnp.concatenate(top_i, axis=0)

    sel = jnp.logical_and(sel, real_ref[...] > 0)
    carry_ref[...] = carry_ref[...] + jnp.sum(sel.astype(F32), axis=1, keepdims=True)
    cnt_ref[...] = carry_ref[...]


def _mix_route(h, ys, ya, wglu, bglu, gs, wout, gf, wrh, wrl, br, real, tm):
    tp, d = h.shape
    full = lambda shape: pl.BlockSpec(shape, lambda i: (0,) * len(shape))
    return pl.pallas_call(
        _mix_kernel,
        grid=(tp // tm,),
        in_specs=[
            pl.BlockSpec((tm, d), lambda i: (i, 0)),
            pl.BlockSpec((N_BUNDLES, tm, LANES), lambda i: (0, i, 0)),
            pl.BlockSpec((tm, D_ATTN), lambda i: (i, 0)),
            full((D_SSM, D_SSM)), full((1, D_SSM)), full((1, D_SSM)),
            full((d, d)), full((1, d)),
            full((N_EXPERTS, d)), full((N_EXPERTS, d)), full((N_EXPERTS, 1)),
            pl.BlockSpec((1, tm), lambda i: (0, i)),
        ],
        out_specs=[
            pl.BlockSpec((tm, d), lambda i: (i, 0)),
            pl.BlockSpec((tm, d), lambda i: (i, 0)),
            pl.BlockSpec((TOP_K, tm), lambda i: (0, i)),
            pl.BlockSpec((TOP_K, tm), lambda i: (0, i)),
            pl.BlockSpec((N_EXPERTS, LANES), lambda i: (0, 0)),
        ],
        out_shape=[
            jax.ShapeDtypeStruct((tp, d), F32),
            jax.ShapeDtypeStruct((tp, d), F32),
            jax.ShapeDtypeStruct((TOP_K, tp), I32),
            jax.ShapeDtypeStruct((TOP_K, tp), F32),
            jax.ShapeDtypeStruct((N_EXPERTS, LANES), F32),
        ],
        scratch_shapes=[pltpu.VMEM((N_EXPERTS, LANES), F32)],
        compiler_params=_cparams("arbitrary"),
        name="mix_route",
    )(h, ys, ya, wglu, bglu, gs, wout, gf, wrh, wrl, br, real)


def _row_dma(src_ref, src_row, dst_ref, dst_row, sem):
    return pltpu.make_async_copy(src_ref.at[pl.ds(src_row, 1), :], dst_ref.at[pl.ds(dst_row, 1), :], sem)


def _expert_kernel(wt_ref, we_ref, lo_ref, hi_ref, tok_ref, tok_next_ref, pair_ref, pair_prev_ref,
                   wg_ref, wu_ref, wd_ref, bg_ref, bu_ref, bd_ref, xn_ref,
                   yp_ref, wgb_ref, wub_ref, wdb_ref, xbuf_ref, ybuf_ref, gsem, ssem, *, pad_blocks):
    w = pl.program_id(0)
    last_w = pl.num_programs(0) - 1
    tile = wt_ref[w]
    slot = lax.rem(tile, 2)
    other = 1 - slot
    prev = jnp.maximum(w - 1, 0)
    lo, hi = lo_ref[w], hi_ref[w]
    new_expert = jnp.logical_or(w == 0, we_ref[w] != we_ref[prev])
    first = jnp.logical_or(w == 0, wt_ref[prev] != tile)
    rows = range(EXPERT_TILE)

    def gather(tokens_ref, s, which=rows):
        for r in which:
            _row_dma(xn_ref, tokens_ref[0, r], xbuf_ref.at[s], r, gsem.at[s]).start(priority=r % 2)

    def scatter(s, pairs_ref, which=rows):
        for r in which:
            _row_dma(ybuf_ref.at[s], r, yp_ref, pairs_ref[0, r], ssem.at[s]).start(priority=r % 2)

    def wait_gather(s):
        for r in rows:
            _row_dma(xn_ref, 0, xbuf_ref.at[s], 0, gsem.at[s]).wait()

    def wait_scatter(s):
        for r in rows:
            _row_dma(ybuf_ref.at[s], 0, yp_ref, 0, ssem.at[s]).wait()

    @pl.when(w == 0)
    def _():
        gather(tok_ref, slot)
        ybuf_ref[...] = jnp.zeros_like(ybuf_ref)
        zero_fills = [pltpu.make_async_copy(ybuf_ref.at[other, pl.ds(0, n), :], yp_ref.at[pl.ds(start, n), :],
                                            ssem.at[other]) for start, n in pad_blocks]
        for fill in zero_fills:
            fill.start()
        for fill in zero_fills:
            fill.wait()
        scatter(slot, pair_ref)

    @pl.when(new_expert)
    def _():
        wgb_ref[...] = wg_ref[...].astype(BF16)
        wub_ref[...] = wu_ref[...].astype(BF16)
        wdb_ref[...] = wd_ref[...].astype(BF16)

    def item(is_first):
        if is_first:
            wait_gather(slot)
            wait_scatter(slot)
        x = xbuf_ref[slot].astype(BF16)
        g = jnp.minimum(jnp.dot(x, wgb_ref[...], preferred_element_type=F32) + bg_ref[...], SWIGLU_LIMIT)
        u = jnp.clip(jnp.dot(x, wub_ref[...], preferred_element_type=F32) + bu_ref[...],
                     -SWIGLU_LIMIT, SWIGLU_LIMIT)
        hm = (g * _sigmoid(SWIGLU_ALPHA * g) * (u + 1.0)).astype(BF16)
        if is_first:
            scatter(other, pair_prev_ref)
            gather(tok_next_ref, other)
        y = jnp.dot(hm, wdb_ref[...], preferred_element_type=F32) + bd_ref[...]
        row = lax.broadcasted_iota(I32, (EXPERT_TILE, 1), 0)
        mine = jnp.logical_and(row >= lo, row < hi)
        ybuf_ref[slot] = jnp.where(mine, y, 0.0 if is_first else ybuf_ref[slot])

    pl.when(first)(functools.partial(item, True))
    pl.when(jnp.logical_not(first))(functools.partial(item, False))

    @pl.when(w == last_w)
    def _():
        scatter(slot, pair_ref)
        wait_scatter(slot)
        wait_scatter(other)
        wait_gather(other)


def _experts(item_tile, item_expert, item_lo, item_hi, tokens, pairs, xn, wg, wu, wd, bg, bu, bd, layer,
             n_pairs, pad_blocks):
    n_tiles = tokens.shape[0]
    assert all(n <= EXPERT_TILE for _, n in pad_blocks)
    d = xn.shape[1]
    dff = wg.shape[3]
    wspec = lambda r, c: pl.BlockSpec((None, None, r, c), lambda w, wt, we, lo, hi: (layer, we[w], 0, 0))
    ids = lambda shift: pl.BlockSpec(
        (None, 1, EXPERT_TILE), lambda w, wt, we, lo, hi: (jnp.clip(wt[w] + shift, 0, n_tiles - 1), 0, 0),
        memory_space=pltpu.SMEM)
    return pl.pallas_call(
        functools.partial(_expert_kernel, pad_blocks=pad_blocks),
        grid_spec=pltpu.PrefetchScalarGridSpec(
            num_scalar_prefetch=4,
            grid=(item_tile.shape[0],),
            in_specs=[
                ids(0), ids(1), ids(0), ids(-1),
                wspec(d, dff), wspec(d, dff), wspec(dff, d),
                wspec(1, dff), wspec(1, dff), wspec(1, d),
                pl.BlockSpec(memory_space=pl.ANY),
            ],
            out_specs=pl.BlockSpec(memory_space=pl.ANY),
            scratch_shapes=[pltpu.VMEM((d, dff), BF16), pltpu.VMEM((d, dff), BF16), pltpu.VMEM((dff, d), BF16),
                            pltpu.VMEM((2, EXPERT_TILE, d), F32), pltpu.VMEM((2, EXPERT_TILE, d), F32),
                            pltpu.SemaphoreType.DMA((2,)), pltpu.SemaphoreType.DMA((2,))],
        ),
        out_shape=jax.ShapeDtypeStruct((n_pairs, d), F32),
        compiler_params=_cparams("arbitrary"),
        name="moe_experts",
    )(item_tile, item_expert, item_lo, item_hi, tokens, tokens, pairs, pairs, wg, wu, wd, bg, bu, bd, xn)


def _combine_kernel(h_ref, w_ref, g_ref, yp_ref, o_ref, *, final_norm):
    out = h_ref[...]
    for k in range(TOP_K):
        out = out + w_ref[:, k:k + 1] * yp_ref[k]
    if final_norm:
        out = _rms(out, g_ref[...])
    o_ref[...] = out


def _combine(h, wcol, g, yp, tm, final_norm):
    tp, d = h.shape
    return pl.pallas_call(
        functools.partial(_combine_kernel, final_norm=final_norm),
        grid=(tp // tm,),
        in_specs=[
            pl.BlockSpec((tm, d), lambda i: (i, 0)),
            pl.BlockSpec((tm, TOP_K), lambda i: (i, 0)),
            pl.BlockSpec((1, d), lambda i: (0, 0)),
            pl.BlockSpec((TOP_K, tm, d), lambda i: (0, i, 0)),
        ],
        out_specs=pl.BlockSpec((tm, d), lambda i: (i, 0)),
        out_shape=jax.ShapeDtypeStruct((tp, d), F32),
        compiler_params=_cparams("parallel"),
        name="moe_combine",
    )(h, wcol, g, yp)


def _routing_tables(cnt, idx, real, n_slots, n_items):
    cnt = cnt.astype(I32)
    end = jnp.cumsum(cnt)
    offs = end - cnt
    experts = jnp.arange(N_EXPERTS, dtype=I32)

    def lookup(table, e):
        hit = e[None] == experts.reshape((N_EXPERTS,) + (1,) * e.ndim)
        return jnp.sum(jnp.where(hit, table.reshape(hit.shape[:1] + (1,) * e.ndim), 0), axis=0)

    tp = idx.shape[1]
    n_pairs = TOP_K * tp
    group = jnp.where(jnp.tile(real.reshape(tp), TOP_K) > 0, idx.reshape(n_pairs), N_EXPERTS)
    pair = jnp.sort(group * n_pairs + jnp.arange(n_pairs, dtype=I32))[:n_slots] % n_pairs
    pairs = pair.reshape(n_slots // EXPERT_TILE, 1, EXPERT_TILE)
    tokens = pairs % tp
    first_tile = offs // EXPERT_TILE
    n_e = jnp.where(cnt > 0, (end - 1) // EXPERT_TILE - first_tile + 1, 0)
    cum_items = jnp.cumsum(n_e)
    w = jnp.minimum(jnp.arange(n_items, dtype=I32), cum_items[-1] - 1)
    e_w = jnp.minimum(jnp.sum((w[None] >= cum_items[:, None]).astype(I32), axis=0), N_EXPERTS - 1)
    tile_w = lookup(first_tile, e_w) + (w - lookup(cum_items - n_e, e_w))
    base = tile_w * EXPERT_TILE
    lo_w = jnp.maximum(lookup(offs, e_w), base) - base
    hi_w = jnp.minimum(lookup(end, e_w), base + EXPERT_TILE) - base
    hi_w = jnp.where(jnp.arange(n_items) < cum_items[-1], hi_w, lo_w)
    return tokens, pairs, tile_w.astype(I32), e_w, lo_w.astype(I32), hi_w.astype(I32)


def kernel(x, meta_tokens, norm_mix, w_in, ssm_lambda_re, ssm_lambda_im, ssm_log_dt, ssm_b_re, ssm_b_im, ssm_c_re, ssm_c_im, ssm_d, w_glu, b_glu, b_forget, norm_ssm_out, norm_attn_out, w_out, norm_ffn, w_router, b_router, w_gate, b_gate, w_up, b_up, w_down, b_down, norm_final):
    bsz, seq, d = x.shape
    depth = w_in.shape[0]
    l_real = N_META + seq
    lp = -(-l_real // SEQ_ALIGN) * SEQ_ALIGN
    tp = bsz * lp
    nc = lp // SSM_CHUNK
    tm_proj = 512 if tp % 512 == 0 else 256
    tm_moe = 256
    n_pairs = TOP_K * tp
    n_slots = -(-(TOP_K * bsz * l_real) // EXPERT_TILE) * EXPERT_TILE
    n_items = n_slots // EXPERT_TILE + N_EXPERTS
    o_q, o_k, o_v, o_f = D_SSM, D_SSM + D_ATTN, D_SSM + 2 * D_ATTN, D_SSM + 3 * D_ATTN
    real = (jnp.arange(tp, dtype=I32) % lp < l_real).astype(I32).reshape(1, tp)
    pad_blocks = tuple((k * tp + b * lp + l_real, lp - l_real)
                       for k in range(TOP_K) for b in range(bsz) if lp > l_real)

    meta = jnp.broadcast_to(meta_tokens[None].astype(x.dtype), (bsz, N_META, d))
    pad = jnp.zeros((bsz, lp - l_real, d), x.dtype)
    h = jnp.concatenate([meta, x, pad], axis=1).reshape(tp, d)
    expert_bias = lambda b: b.reshape(depth, N_EXPERTS, 1, b.shape[-1])
    b_gate4, b_up4, b_down4 = expert_bias(b_gate), expert_bias(b_up), expert_bias(b_down)

    for l in range(depth):
        w_l = w_in[l]
        w_k = jnp.pad(w_l[:, o_k:o_v].reshape(d, N_HEADS, HEAD_DIM), ((0, 0), (0, 0), (0, AUG_ROWS)))
        w_a = jnp.concatenate([w_l[:, :o_q], w_k.reshape(d, N_HEADS * LANES),
                               jnp.pad(w_l[:, o_f:], ((0, 0), (0, LANES - N_HEADS)))], axis=1).astype(BF16)
        w_bt = jnp.concatenate([w_l[:, o_q:o_k], w_l[:, o_v:o_f]], axis=1).T.astype(BF16)
        bf = jnp.pad(b_forget[l].astype(F32), (0, LANES - N_HEADS)).reshape(1, LANES)
        u, ka, qa, vt = _in_proj(h.reshape(bsz, lp, d), norm_mix[l].reshape(1, d), w_a, w_bt, bf,
                                 SEQ_ALIGN, SEQ_ALIGN)

        wt, wb, wc, a16 = _ssm_operators(ssm_lambda_re[l], ssm_lambda_im[l], ssm_log_dt[l], ssm_b_re[l],
                                         ssm_b_im[l], ssm_c_re[l], ssm_c_im[l], ssm_d[l])
        ys = _ssm(u, wt, wb, wc, a16).reshape(N_BUNDLES, tp, LANES)

        ya = _attention(qa, ka, vt, norm_attn_out[l].reshape(1, D_ATTN), SEQ_ALIGN).reshape(tp, D_ATTN)

        wr = w_router[l].T.astype(F32)
        wrh = wr.astype(BF16)
        wrl = (wr - wrh.astype(F32)).astype(BF16)
        h, xn, idx, wts, cnt = _mix_route(
            h, ys, ya, w_glu[l].astype(BF16), b_glu[l].reshape(1, D_SSM), norm_ssm_out[l].reshape(1, D_SSM),
            w_out[l].astype(BF16), norm_ffn[l].reshape(1, d), wrh, wrl,
            b_router[l].astype(F32).reshape(N_EXPERTS, 1), real, tm_proj)

        tokens, pairs, item_tile, item_expert, item_lo, item_hi = _routing_tables(
            cnt[:, 0], idx, real, n_slots, n_items)
        yp = _experts(item_tile, item_expert, item_lo, item_hi, tokens, pairs, xn, w_gate, w_up, w_down,
                      b_gate4, b_up4, b_down4, l, n_pairs, pad_blocks)
        h = _combine(h, wts.T, norm_final.reshape(1, d), yp.reshape(TOP_K, tp, d), tm_moe, l == depth - 1)

    return h.reshape(bsz, lp, d)[:, N_META:l_real]
```
